```python
import jax
import jax.numpy as jnp
from jax import lax
import numpy as np

D_MODEL = 2048
BATCH = 4
SEQ = 4096
DEPTH = 1

GRID_W = 64
CTX_LEN = 256
HEAD_DIM = 128
N_Q_HEADS = 16
N_KV_HEADS = 4
Q_PER_KV = N_Q_HEADS // N_KV_HEADS
ATTN_WIDTH = N_Q_HEADS * HEAD_DIM
KV_WIDTH = N_KV_HEADS * HEAD_DIM
CONV_WIDTH = D_MODEL // 2
CONV_K = 3
N_BRANCHES = 2
KV_LO = ATTN_WIDTH
KV_HI = ATTN_WIDTH + 2 * KV_WIDTH
IN_SPLITS = (ATTN_WIDTH, ATTN_WIDTH + KV_WIDTH, KV_HI,
             KV_HI + CONV_WIDTH, KV_HI + 2 * CONV_WIDTH, KV_HI + 3 * CONV_WIDTH,
             KV_HI + 3 * CONV_WIDTH + D_MODEL)
IN_WIDTH = KV_HI + 3 * CONV_WIDTH + N_BRANCHES * D_MODEL
Q_BLOCK = 128
ROPE_THETA = 10000.0
ROPE_PAIRS_PER_AXIS = HEAD_DIM // 4
ATTN_SCALE = HEAD_DIM ** -0.5
N_GROUPS = 8
EXPERTS_PER_GROUP = 8
N_EXPERTS = N_GROUPS * EXPERTS_PER_GROUP
TOP_K = 2
D_EXPERT = 512
EXPERT_BLOCK = 128
N_MOD = 6
EPS = 1e-6

kernel_name = 'hybrid_dit_gqa_shortconv_hmoe'


def rms_norm(t, g):
    tf = t.astype(jnp.float32)
    tf = tf * lax.rsqrt(jnp.mean(tf * tf, axis=-1, keepdims=True) + EPS)
    return (tf * g.astype(jnp.float32)).astype(t.dtype)


def modulate(t, shift, scale):
    return t * (1 + scale) + shift


def split_heads(t, n_heads):
    return t.reshape(t.shape[:-1] + (n_heads, HEAD_DIM))


def axial_rope_tables(n_rows):
    row = jnp.repeat(jnp.arange(n_rows, dtype=jnp.float32), GRID_W)
    col = jnp.tile(jnp.arange(GRID_W, dtype=jnp.float32), n_rows)
    freqs = ROPE_THETA ** (-jnp.arange(ROPE_PAIRS_PER_AXIS, dtype=jnp.float32) / ROPE_PAIRS_PER_AXIS)
    ang = jnp.concatenate([row[:, None] * freqs, col[:, None] * freqs], axis=-1)
    return jnp.cos(ang)[:, None, :], jnp.sin(ang)[:, None, :]


def apply_rope(t, cos, sin):
    tf = t.astype(jnp.float32)
    t1, t2 = jnp.split(tf, 2, axis=-1)
    return jnp.concatenate([t1 * cos - t2 * sin, t2 * cos + t1 * sin], axis=-1).astype(t.dtype)


def gqa_attend(q5, k, v):
    s = jnp.einsum('bqhgd,bkhd->bhgqk', q5, k).astype(jnp.float32) * ATTN_SCALE
    p = jax.nn.softmax(s, axis=-1).astype(v.dtype)
    return jnp.einsum('bhgqk,bkhd->bqhgd', p, v)


def latent_attention(q, k, v, k_ctx, v_ctx):
    b, s = q.shape[0], q.shape[1]
    k_all = jnp.concatenate([k_ctx, k], axis=1)
    v_all = jnp.concatenate([v_ctx, v], axis=1)
    qb = q.reshape(b, s // Q_BLOCK, Q_BLOCK, N_KV_HEADS, Q_PER_KV, HEAD_DIM).transpose(1, 0, 2, 3, 4, 5)
    ob = lax.map(lambda blk: gqa_attend(blk, k_all, v_all), qb)
    return ob.transpose(1, 0, 2, 3, 4, 5).reshape(b, s, ATTN_WIDTH)


def context_attention(q_c, k_c, v_c):
    b, n = q_c.shape[0], q_c.shape[1]
    q5 = q_c.reshape(b, n, N_KV_HEADS, Q_PER_KV, HEAD_DIM)
    return gqa_attend(q5, k_c, v_c).reshape(b, n, ATTN_WIDTH)


def short_conv(u, w):
    n = u.shape[1]
    half = CONV_K // 2
    up = jnp.pad(u, ((0, 0), (half, half), (0, 0)))
    return sum(up[:, j:j + n] * w[j] for j in range(CONV_K))


def merge_branches(attn, pb, pc, px, pga, pgc, conv_w, w_attn_out, w_conv_out, w_mix_out):
    y_attn = attn @ w_attn_out
    y_conv = (pb * short_conv(pc * px, conv_w)) @ w_conv_out
    merged = jax.nn.sigmoid(pga) * y_attn + jax.nn.sigmoid(pgc) * y_conv
    return merged @ w_mix_out


def hierarchical_moe(t, w_rg, b_rg, w_re, b_re, w_gate, w_up, w_down):
    n_tok, d = t.shape
    tf = t.astype(jnp.float32)
    grp_prob = jax.nn.softmax(tf @ w_rg.astype(jnp.float32) + b_rg.astype(jnp.float32), axis=-1)
    grp_p, grp_idx = lax.top_k(grp_prob, 1)
    exp_logits = jnp.einsum('nd,dge->nge', tf, w_re.astype(jnp.float32)) + b_re.astype(jnp.float32)
    exp_logits = jnp.take_along_axis(exp_logits, grp_idx[:, :, None], axis=1)[:, 0]
    top_p, top_i = lax.top_k(jax.nn.softmax(exp_logits, axis=-1), TOP_K)
    gate = grp_p * top_p / jnp.sum(top_p, axis=-1, keepdims=True)
    expert = grp_idx * EXPERTS_PER_GROUP + top_i
    flat_e = expert.reshape(-1)
    flat_w = gate.reshape(-1)
    flat_tok = jnp.repeat(jnp.arange(n_tok, dtype=jnp.int32), TOP_K)
    n_assign = flat_e.shape[0]
    order = jnp.argsort(flat_e)
    e_sorted = flat_e[order]
    counts = jnp.bincount(flat_e, length=N_EXPERTS)
    padded = (counts + EXPERT_BLOCK - 1) // EXPERT_BLOCK * EXPERT_BLOCK
    start = jnp.cumsum(counts) - counts
    pstart = jnp.cumsum(padded) - padded
    dest = pstart[e_sorted] + (jnp.arange(n_assign, dtype=jnp.int32) - start[e_sorted])
    n_blocks = -(-n_assign // EXPERT_BLOCK) + N_EXPERTS
    m_pad = n_blocks * EXPERT_BLOCK
    row_tok = jnp.full((m_pad,), n_tok, jnp.int32).at[dest].set(flat_tok[order])
    row_w = jnp.zeros((m_pad,), jnp.float32).at[dest].set(flat_w[order])
    block_start = jnp.arange(n_blocks, dtype=jnp.int32) * EXPERT_BLOCK
    block_expert = jnp.minimum(jnp.searchsorted(pstart + padded, block_start, side='right'), N_EXPERTS - 1)
    t_pad = jnp.concatenate([t, jnp.zeros((1, d), t.dtype)], axis=0)
    xb = t_pad[row_tok].reshape(n_blocks, EXPERT_BLOCK, d)

    def expert_block(args):
        xe, e = args
        h = jax.nn.silu(xe @ w_gate[e]) * (xe @ w_up[e])
        return h @ w_down[e]

    yb = lax.map(expert_block, (xb, block_expert))
    y = yb.reshape(m_pad, d) * row_w[:, None].astype(t.dtype)
    return jax.ops.segment_sum(y, row_tok, num_segments=n_tok + 1)[:n_tok]


def setup_inputs(seed: int = 0) -> dict:
    key = jax.random.key(seed)
    ks = jax.random.split(key, 24)

    def normal(k, shape, scale):
        return jax.random.normal(k, shape, jnp.float32) * scale

    def gain(k, shape):
        return 1.0 + 0.02 * jax.random.normal(k, shape, jnp.float32)

    return {
        'x': normal(ks[0], (BATCH, SEQ, D_MODEL), 1.0),
        'c': normal(ks[1], (BATCH, D_MODEL), 1.0),
        'ctx': normal(ks[2], (BATCH, CTX_LEN, D_MODEL), 1.0),
        'c_ctx': normal(ks[3], (D_MODEL,), 1.0),
        'w_ada': normal(ks[4], (DEPTH, D_MODEL, N_MOD * D_MODEL), 0.5 * D_MODEL ** -0.5),
        'b_ada': normal(ks[5], (DEPTH, N_MOD * D_MODEL), 0.01),
        'g_pre_mix': gain(ks[6], (DEPTH, D_MODEL)),
        'w_in': normal(ks[7], (DEPTH, D_MODEL, IN_WIDTH), D_MODEL ** -0.5),
        'q_norm': gain(ks[8], (DEPTH, HEAD_DIM)),
        'k_norm': gain(ks[9], (DEPTH, HEAD_DIM)),
        'conv_w': normal(ks[10], (DEPTH, CONV_K, CONV_WIDTH), CONV_K ** -0.5),
        'w_attn_out': normal(ks[11], (DEPTH, ATTN_WIDTH, D_MODEL), ATTN_WIDTH ** -0.5),
        'w_conv_out': normal(ks[12], (DEPTH, CONV_WIDTH, D_MODEL), CONV_WIDTH ** -0.5),
        'w_mix_out': normal(ks[13], (DEPTH, D_MODEL, D_MODEL), D_MODEL ** -0.5),
        'g_post_mix': gain(ks[14], (DEPTH, D_MODEL)),
        'g_pre_ffn': gain(ks[15], (DEPTH, D_MODEL)),
        'w_router_group': normal(ks[16], (DEPTH, D_MODEL, N_GROUPS), D_MODEL ** -0.5),
        'b_router_group': normal(ks[17], (DEPTH, N_GROUPS), 0.01),
        'w_router_expert': normal(ks[18], (DEPTH, D_MODEL, N_GROUPS, EXPERTS_PER_GROUP), D_MODEL ** -0.5),
        'b_router_expert': normal(ks[19], (DEPTH, N_GROUPS, EXPERTS_PER_GROUP), 0.01),
        'w_gate': normal(ks[20], (DEPTH, N_EXPERTS, D_MODEL, D_EXPERT), D_MODEL ** -0.5),
        'w_up': normal(ks[21], (DEPTH, N_EXPERTS, D_MODEL, D_EXPERT), D_MODEL ** -0.5),
        'w_down': normal(ks[22], (DEPTH, N_EXPERTS, D_EXPERT, D_MODEL), D_EXPERT ** -0.5),
        'g_post_ffn': gain(ks[23], (DEPTH, D_MODEL)),
    }


def reference(x, c, ctx, c_ctx, w_ada, b_ada, g_pre_mix, w_in, q_norm, k_norm, conv_w,
              w_attn_out, w_conv_out, w_mix_out, g_post_mix, g_pre_ffn, w_router_group,
              b_router_group, w_router_expert, b_router_expert, w_gate, w_up, w_down, g_post_ffn):
    b, s, d = x.shape
    n_ctx = ctx.shape[1]
    rows = s // GRID_W
    cos, sin = axial_rope_tables(rows)
    h_lat, h_ctx = x, ctx
    for layer in range(DEPTH):
        last = layer == DEPTH - 1
        mod_lat = jax.nn.silu(c) @ w_ada[layer] + b_ada[layer]
        mod_ctx = jax.nn.silu(c_ctx) @ w_ada[layer] + b_ada[layer]
        sh1, sc1, gt1, sh2, sc2, gt2 = jnp.split(mod_lat[:, None, :], N_MOD, axis=-1)
        csh1, csc1, cgt1, csh2, csc2, cgt2 = jnp.split(mod_ctx, N_MOD, axis=-1)
        w_in_l = w_in[layer]

        a_lat = modulate(rms_norm(h_lat, g_pre_mix[layer]), sh1, sc1)
        a_ctx = modulate(rms_norm(h_ctx, g_pre_mix[layer]), csh1, csc1)
        q, k, v, pb, pc, px, pga, pgc = jnp.split(a_lat @ w_in_l, IN_SPLITS, axis=-1)
        q = apply_rope(rms_norm(split_heads(q, N_Q_HEADS), q_norm[layer]), cos, sin)
        k = apply_rope(rms_norm(split_heads(k, N_KV_HEADS), k_norm[layer]), cos, sin)
        v = split_heads(v, N_KV_HEADS)
        if last:
            k_c, v_c = jnp.split(a_ctx @ w_in_l[:, KV_LO:KV_HI], 2, axis=-1)
        else:
            q_c, k_c, v_c, cb, cc, cx, cga, cgc = jnp.split(a_ctx @ w_in_l, IN_SPLITS, axis=-1)
        k_c = rms_norm(split_heads(k_c, N_KV_HEADS), k_norm[layer])
        v_c = split_heads(v_c, N_KV_HEADS)
        attn_lat = latent_attention(q, k, v, k_c, v_c)
        y_lat = merge_branches(attn_lat, pb, pc, px, pga, pgc, conv_w[layer],
                               w_attn_out[layer], w_conv_out[layer], w_mix_out[layer])
        h_lat = h_lat + gt1 * rms_norm(y_lat, g_post_mix[layer])
        if not last:
            q_c = rms_norm(split_heads(q_c, N_Q_HEADS), q_norm[layer])
            attn_ctx = context_attention(q_c, k_c, v_c)
            y_ctx = merge_branches(attn_ctx, cb, cc, cx, cga, cgc, conv_w[layer],
                                   w_attn_out[layer], w_conv_out[layer], w_mix_out[layer])
            h_ctx = h_ctx + cgt1 * rms_norm(y_ctx, g_post_mix[layer])

        f_lat = modulate(rms_norm(h_lat, g_pre_ffn[layer]), sh2, sc2).reshape(b * s, d)
        if last:
            y = hierarchical_moe(f_lat, w_router_group[layer], b_router_group[layer],
                                 w_router_expert[layer], b_router_expert[layer],
                                 w_gate[layer], w_up[layer], w_down[layer])
            h_lat = h_lat + gt2 * rms_norm(y.reshape(b, s, d), g_post_ffn[layer])
        else:
            f_ctx = modulate(rms_norm(h_ctx, g_pre_ffn[layer]), csh2, csc2).reshape(b * n_ctx, d)
            y = hierarchical_moe(jnp.concatenate([f_ctx, f_lat], axis=0),
                                 w_router_group[layer], b_router_group[layer],
                                 w_router_expert[layer], b_router_expert[layer],
                                 w_gate[layer], w_up[layer], w_down[layer])
            y_ctx = y[:b * n_ctx].reshape(b, n_ctx, d)
            y_lat = y[b * n_ctx:].reshape(b, s, d)
            h_ctx = h_ctx + cgt2 * rms_norm(y_ctx, g_post_ffn[layer])
            h_lat = h_lat + gt2 * rms_norm(y_lat, g_post_ffn[layer])
    return h_lat
```

```python
import functools

import jax
import jax.numpy as jnp
from jax import lax
from jax.experimental import pallas as pl
from jax.experimental.pallas import tpu as pltpu

HEAD_DIM = 128
GRID_W = 64
ROPE_THETA = 10000.0
EPS = 1e-6
TOP_K = 2
N_MOD = 6
LANES = 128
BF16_SUBLANES = 16
VMEM_LIMIT_BYTES = 56 * 1024 * 1024
NEG_BIG = -1e30

F32 = jnp.float32
BF16 = jnp.bfloat16


def _params(n_grid_axes):
    return pltpu.CompilerParams(dimension_semantics=("arbitrary",) * n_grid_axes,
                                vmem_limit_bytes=VMEM_LIMIT_BYTES)


def _rms(t, gain):
    ms = jnp.mean(t * t, axis=-1, keepdims=True)
    return t * lax.rsqrt(ms + EPS) * gain


def _adaln_body(c_ref, w_ref, b_ref, o_ref):
    c = c_ref[...]
    s = (c * jax.nn.sigmoid(c)).astype(BF16)
    o_ref[...] = jnp.dot(s, w_ref[...].astype(BF16), preferred_element_type=F32) + b_ref[...]


def _adaln(cc, w, b):
    rows, d = cc.shape
    n = w.shape[1]
    tn = min(1024, n)
    return pl.pallas_call(
        _adaln_body,
        grid=(n // tn,),
        in_specs=[pl.BlockSpec((rows, d), lambda j: (0, 0)),
                  pl.BlockSpec((d, tn), lambda j: (0, j)),
                  pl.BlockSpec((1, tn), lambda j: (0, j))],
        out_specs=pl.BlockSpec((rows, tn), lambda j: (0, j)),
        out_shape=jax.ShapeDtypeStruct((rows, n), F32),
        compiler_params=_params(1),
        name="adaln",
    )(cc, w, b.reshape(1, n))


def _inproj_body(x_ref, mod_ref, g_ref, w_ref, cos_ref, sin_ref, qn_ref, kn_ref, o_ref, a_scr, *,
                 col_off, n_q_tiles, n_k_tiles, gate_start, use_rope, q_scale):
    j = pl.program_id(1)

    @pl.when(j == 0)
    def _():
        xn = _rms(x_ref[...], g_ref[...])
        shift = mod_ref[0, 0:1, :]
        scale = mod_ref[0, 1:2, :]
        a_scr[...] = (xn * (1.0 + scale) + shift).astype(BF16)

    acc = jnp.dot(a_scr[...], w_ref[...], preferred_element_type=F32)
    jj = j + col_off
    heads = acc.shape[1] // HEAD_DIM

    def normed_heads(gain, out_scale):
        for hh in range(heads):
            sl = slice(hh * HEAD_DIM, (hh + 1) * HEAD_DIM)
            t = _rms(acc[:, sl], gain)
            if use_rope:
                t = t * cos_ref[...] + pltpu.roll(t, HEAD_DIM // 2, 1) * sin_ref[...]
            if out_scale != 1.0:
                t = t * out_scale
            o_ref[:, sl] = t.astype(o_ref.dtype)

    @pl.when(jj < n_q_tiles)
    def _():
        normed_heads(qn_ref[...], q_scale)

    @pl.when((jj >= n_q_tiles) & (jj < n_q_tiles + n_k_tiles))
    def _():
        normed_heads(kn_ref[...], 1.0)

    @pl.when((jj >= n_q_tiles + n_k_tiles) & (jj < gate_start))
    def _():
        o_ref[...] = acc.astype(o_ref.dtype)

    @pl.when(jj >= gate_start)
    def _():
        o_ref[...] = jax.nn.sigmoid(acc).astype(o_ref.dtype)


def _inproj(x2, mod3, g, w_b, cos2, sin2, qn, kn, *, tm, tn, col_off, n_col_tiles, mod_row, pos_blocks,
            attn_width, kv_width, gate_start_col, use_rope):
    rows, d = x2.shape
    body = functools.partial(
        _inproj_body, col_off=col_off, n_q_tiles=attn_width // tn, n_k_tiles=kv_width // tn,
        gate_start=gate_start_col // tn, use_rope=use_rope, q_scale=HEAD_DIM ** -0.5)
    return pl.pallas_call(
        body,
        grid=(rows // tm, n_col_tiles),
        in_specs=[pl.BlockSpec((tm, d), lambda i, j: (i, 0)),
                  pl.BlockSpec((1, N_MOD, d), lambda i, j: (mod_row(i), 0, 0)),
                  pl.BlockSpec((1, d), lambda i, j: (0, 0)),
                  pl.BlockSpec((d, tn), lambda i, j: (0, j + col_off)),
                  pl.BlockSpec((tm, HEAD_DIM), lambda i, j: (i % pos_blocks, 0)),
                  pl.BlockSpec((tm, HEAD_DIM), lambda i, j: (i % pos_blocks, 0)),
                  pl.BlockSpec((1, HEAD_DIM), lambda i, j: (0, 0)),
                  pl.BlockSpec((1, HEAD_DIM), lambda i, j: (0, 0))],
        out_specs=pl.BlockSpec((tm, tn), lambda i, j: (i, j)),
        out_shape=jax.ShapeDtypeStruct((rows, n_col_tiles * tn), BF16),
        scratch_shapes=[pltpu.VMEM((tm, d), BF16)],
        compiler_params=_params(2),
        name="inproj_rope" if use_rope else "inproj_ctx",
    )(x2, mod3, g, w_b, cos2, sin2, qn, kn)


def _attn_body(q_ref, k_ref, v_ref, o_ref, *, n_ctx, tk, group):
    tq = q_ref.shape[0]
    n_keys = k_ref.shape[1]
    q = jnp.concatenate([q_ref[:, g * HEAD_DIM:(g + 1) * HEAD_DIM] for g in range(group)], axis=0)
    rows = q.shape[0]

    def chunk(start, size, carry):
        m, l, acc = carry
        k = k_ref[0, pl.ds(start, size), :]
        v = v_ref[0, pl.ds(start, size), :]
        s = lax.dot_general(q, k, (((1,), (1,)), ((), ())), preferred_element_type=F32)
        m_new = jnp.maximum(m, jnp.max(s, axis=-1, keepdims=True))
        alpha = jnp.exp(m - m_new)
        p = jnp.exp(s - m_new)
        l = alpha * l + jnp.sum(p, axis=-1, keepdims=True)
        acc = alpha * acc + jnp.dot(p.astype(BF16), v, preferred_element_type=F32)
        return m_new, l, acc

    carry = (jnp.full((rows, 1), NEG_BIG, F32), jnp.zeros((rows, 1), F32), jnp.zeros((rows, HEAD_DIM), F32))
    carry = chunk(0, n_ctx, carry)
    n_chunks = (n_keys - n_ctx) // tk
    carry = lax.fori_loop(
        0, n_chunks, lambda c, cr: chunk(pl.multiple_of(n_ctx + c * tk, BF16_SUBLANES), tk, cr), carry)
    _, l, acc = carry
    out = acc / l
    for g in range(group):
        o_ref[:, g * HEAD_DIM:(g + 1) * HEAD_DIM] = out[g * tq:(g + 1) * tq].astype(o_ref.dtype)


def _attention(p_all, k_all, v_all, *, batch, seq, n_kv, group, tq, tk, n_ctx):
    gw = group * HEAD_DIM
    q_tiles = seq // tq
    body = functools.partial(_attn_body, n_ctx=n_ctx, tk=tk, group=group)
    n_keys = k_all.shape[1]
    return pl.pallas_call(
        body,
        grid=(batch, n_kv, q_tiles),
        in_specs=[pl.BlockSpec((tq, gw), lambda b, h, i: (b * q_tiles + i, h)),
                  pl.BlockSpec((1, n_keys, HEAD_DIM), lambda b, h, i: (b, 0, h)),
                  pl.BlockSpec((1, n_keys, HEAD_DIM), lambda b, h, i: (b, 0, h))],
        out_specs=pl.BlockSpec((tq, gw), lambda b, h, i: (b * q_tiles + i, h)),
        out_shape=jax.ShapeDtypeStruct((batch * seq, n_kv * gw), BF16),
        compiler_params=_params(3),
        name="attention",
    )(p_all, k_all, v_all)


def _merge_body(attn_ref, pb_ref, pc_ref, px_ref, pcp_ref, pxp_ref, pcn_ref, pxn_ref, ga_ref, gc_ref,
                cw_ref, wa_ref, wc_ref, o_ref, *, tiles_per_seq):
    i = pl.program_id(0)
    tm = attn_ref.shape[0]
    u = pc_ref[...].astype(F32) * px_ref[...].astype(F32)
    last = BF16_SUBLANES - 1
    u_before = pcp_ref[last:last + 1, :].astype(F32) * pxp_ref[last:last + 1, :].astype(F32)
    u_after = pcn_ref[0:1, :].astype(F32) * pxn_ref[0:1, :].astype(F32)
    pos = i % tiles_per_seq
    u_before = jnp.where(pos == 0, 0.0, u_before)
    u_after = jnp.where(pos == tiles_per_seq - 1, 0.0, u_after)
    row = lax.broadcasted_iota(jnp.int32, u.shape, 0)
    u_prev = jnp.where(row == 0, u_before, pltpu.roll(u, 1, 0))
    u_next = jnp.where(row == tm - 1, u_after, pltpu.roll(u, tm - 1, 0))
    conv = cw_ref[0:1, :] * u_prev + cw_ref[1:2, :] * u + cw_ref[2:3, :] * u_next
    z = (pb_ref[...].astype(F32) * conv).astype(BF16)
    y_conv = jnp.dot(z, wc_ref[...], preferred_element_type=F32)
    y_attn = jnp.dot(attn_ref[...], wa_ref[...], preferred_element_type=F32)
    merged = ga_ref[...].astype(F32) * y_attn + gc_ref[...].astype(F32) * y_conv
    o_ref[...] = merged.astype(o_ref.dtype)


def _merge(attn, p_all, conv_w, wa_b, wc_b, *, tm, seq, conv_width, d, conv_start_col, gate_start_col):
    rows, attn_width = attn.shape
    hb = BF16_SUBLANES
    cb = conv_start_col // conv_width
    gb = gate_start_col // d
    n_hblocks = rows // hb
    per = tm // hb

    def prev_map(col):
        return lambda i: (jnp.maximum(i * per - 1, 0), col)

    def next_map(col):
        return lambda i: (jnp.minimum((i + 1) * per, n_hblocks - 1), col)

    body = functools.partial(_merge_body, tiles_per_seq=seq // tm)
    resident = dict(pipeline_mode=pl.Buffered(1))
    return pl.pallas_call(
        body,
        grid=(rows // tm,),
        in_specs=[pl.BlockSpec((tm, attn_width), lambda i: (i, 0)),
                  pl.BlockSpec((tm, conv_width), lambda i: (i, cb)),
                  pl.BlockSpec((tm, conv_width), lambda i: (i, cb + 1)),
                  pl.BlockSpec((tm, conv_width), lambda i: (i, cb + 2)),
                  pl.BlockSpec((hb, conv_width), prev_map(cb + 1)),
                  pl.BlockSpec((hb, conv_width), prev_map(cb + 2)),
                  pl.BlockSpec((hb, conv_width), next_map(cb + 1)),
                  pl.BlockSpec((hb, conv_width), next_map(cb + 2)),
                  pl.BlockSpec((tm, d), lambda i: (i, gb)),
                  pl.BlockSpec((tm, d), lambda i: (i, gb + 1)),
                  pl.BlockSpec(conv_w.shape, lambda i: (0, 0)),
                  pl.BlockSpec(wa_b.shape, lambda i: (0, 0), **resident),
                  pl.BlockSpec(wc_b.shape, lambda i: (0, 0), **resident)],
        out_specs=pl.BlockSpec((tm, d), lambda i: (i, 0)),
        out_shape=jax.ShapeDtypeStruct((rows, d), BF16),
        compiler_params=_params(1),
        name="merge",
    )(attn, p_all, p_all, p_all, p_all, p_all, p_all, p_all, p_all, p_all, conv_w, wa_b, wc_b)


def _split_bf16(t):
    hi = t.astype(BF16)
    lo = (t - hi.astype(F32)).astype(BF16)
    return hi, lo


def _mix_route_body(m_ref, wm_ref, x_ref, mod_ref, gpost_ref, gpre_ref, wr_ref, br_ref,
                    h_ref, f_ref, route_ref, cnt_ref, carry_scr, *, n_groups, per_group):
    i = pl.program_id(0)
    tm = m_ref.shape[0]

    @pl.when(i == 0)
    def _():
        carry_scr[...] = jnp.zeros_like(carry_scr)

    y = jnp.dot(m_ref[...], wm_ref[...], preferred_element_type=F32)
    h = x_ref[...] + mod_ref[0, 2:3, :] * _rms(y, gpost_ref[...])
    h_ref[...] = h
    f = _rms(h, gpre_ref[...]) * (1.0 + mod_ref[0, 4:5, :]) + mod_ref[0, 3:4, :]
    f_ref[...] = f

    f_hi, f_lo = _split_bf16(f)
    w_hi, w_lo = _split_bf16(wr_ref[...])
    logits = (jnp.dot(f_hi, w_hi, preferred_element_type=F32)
              + jnp.dot(f_hi, w_lo, preferred_element_type=F32)
              + jnp.dot(f_lo, w_hi, preferred_element_type=F32)) + br_ref[...]

    lane = lax.broadcasted_iota(jnp.int32, logits.shape, 1)

    def first_argmax(vals):
        mx = jnp.max(vals, axis=-1, keepdims=True)
        idx = jnp.min(jnp.where(vals == mx, lane, LANES), axis=-1, keepdims=True)
        return mx, idx

    is_group = lane < n_groups
    g_max, g_idx = first_argmax(jnp.where(is_group, logits, NEG_BIG))
    g_den = jnp.sum(jnp.where(is_group, jnp.exp(logits - g_max), 0.0), axis=-1, keepdims=True)
    grp_p = 1.0 / g_den
    lo_lane = n_groups + g_idx * per_group
    e_logits = jnp.where((lane >= lo_lane) & (lane < lo_lane + per_group), logits, NEG_BIG)
    l1, i1 = first_argmax(e_logits)
    l2, i2 = first_argmax(jnp.where(lane == i1, NEG_BIG, e_logits))
    r = jnp.exp(l2 - l1)
    gate1 = grp_p / (1.0 + r)
    gate2 = grp_p * r / (1.0 + r)
    e1 = i1 - n_groups
    e2 = i2 - n_groups

    onehot = ((lane == e1) | (lane == e2)).astype(BF16)
    r_i = lax.broadcasted_iota(jnp.int32, (tm, tm), 0)
    c_i = lax.broadcasted_iota(jnp.int32, (tm, tm), 1)
    before = (c_i < r_i).astype(BF16)
    seen = jnp.dot(before, onehot, preferred_element_type=F32) + carry_scr[0:1, :]
    rank1 = jnp.sum(jnp.where(lane == e1, seen, 0.0), axis=-1, keepdims=True)
    rank2 = jnp.sum(jnp.where(lane == e2, seen, 0.0), axis=-1, keepdims=True)
    carry_scr[...] = carry_scr[...] + jnp.sum(onehot.astype(F32), axis=0, keepdims=True)
    cnt_ref[...] = carry_scr[...]

    route = jnp.where(lane == 0, e1.astype(F32), 0.0)
    route = jnp.where(lane == 1, e2.astype(F32), route)
    route = jnp.where(lane == 2, gate1, route)
    route = jnp.where(lane == 3, gate2, route)
    route = jnp.where(lane == 4, rank1, route)
    route = jnp.where(lane == 5, rank2, route)
    route_ref[...] = route


def _mix_route(merged, wm_b, x2, mod3, gpost, gpre, w_r, b_r, *, tm, seq, n_groups, per_group):
    rows, d = x2.shape
    per_batch = seq // tm
    body = functools.partial(_mix_route_body, n_groups=n_groups, per_group=per_group)
    return pl.pallas_call(
        body,
        grid=(rows // tm,),
        in_specs=[pl.BlockSpec((tm, d), lambda i: (i, 0)),
                  pl.BlockSpec(wm_b.shape, lambda i: (0, 0), pipeline_mode=pl.Buffered(1)),
                  pl.BlockSpec((tm, d), lambda i: (i, 0)),
                  pl.BlockSpec((1, N_MOD, d), lambda i: (i // per_batch, 0, 0)),
                  pl.BlockSpec((1, d), lambda i: (0, 0)),
                  pl.BlockSpec((1, d), lambda i: (0, 0)),
                  pl.BlockSpec((d, LANES), lambda i: (0, 0)),
                  pl.BlockSpec((1, LANES), lambda i: (0, 0))],
        out_specs=[pl.BlockSpec((tm, d), lambda i: (i, 0)),
                   pl.BlockSpec((tm, d), lambda i: (i, 0)),
                   pl.BlockSpec((tm, LANES), lambda i: (i, 0)),
                   pl.BlockSpec((8, LANES), lambda i: (0, 0))],
        out_shape=[jax.ShapeDtypeStruct((rows, d), F32),
                   jax.ShapeDtypeStruct((rows, d), F32),
                   jax.ShapeDtypeStruct((rows, LANES), F32),
                   jax.ShapeDtypeStruct((8, LANES), F32)],
        scratch_shapes=[pltpu.VMEM((8, LANES), F32)],
        compiler_params=_params(1),
        name="mix_route",
    )(merged, wm_b, x2, mod3, gpost, gpre, w_r, b_r)


def _experts_body(be_ref, nu_ref, tok_ref, f_hbm, wg_ref, wu_ref, wd_ref, o_ref,
                  xbuf, sem, wg_b, wu_b, wd_b, *, bm):
    i = pl.program_id(0)
    n_used = nu_ref[0]

    def start_gather(blk, slot):
        base = blk * bm

        def body(r, carry):
            tok = tok_ref[base + r]
            pltpu.make_async_copy(f_hbm.at[pl.ds(tok, 1)], xbuf.at[slot, pl.ds(r, 1)], sem.at[slot]).start()
            return carry

        lax.fori_loop(0, bm, body, 0)

    def wait_gather(slot):
        def body(r, carry):
            pltpu.make_async_copy(f_hbm.at[pl.ds(0, 1)], xbuf.at[slot, pl.ds(r, 1)], sem.at[slot]).wait()
            return carry

        lax.fori_loop(0, bm, body, 0)

    @pl.when(i == 0)
    def _():
        start_gather(0, 0)

    @pl.when(i + 1 < n_used)
    def _():
        start_gather(i + 1, (i + 1) % 2)

    @pl.when(i < n_used)
    def _():
        slot = i % 2
        changed = (i == 0) | (be_ref[i] != be_ref[jnp.maximum(i - 1, 0)])

        @pl.when(changed)
        def _():
            wg_b[...] = wg_ref[0].astype(BF16)
            wu_b[...] = wu_ref[0].astype(BF16)
            wd_b[...] = wd_ref[0].astype(BF16)

        wait_gather(slot)
        x = xbuf[slot].astype(BF16)
        g = jnp.dot(x, wg_b[...], preferred_element_type=F32)
        u = jnp.dot(x, wu_b[...], preferred_element_type=F32)
        hid = (g * jax.nn.sigmoid(g) * u).astype(BF16)
        o_ref[...] = jnp.dot(hid, wd_b[...], preferred_element_type=F32)


def _experts(block_expert, n_used, row_tok, f, w_gate, w_up, w_down, *, bm):
    n_blocks = block_expert.shape[0]
    _, d, de = w_gate.shape
    body = functools.partial(_experts_body, bm=bm)
    grid_spec = pltpu.PrefetchScalarGridSpec(
        num_scalar_prefetch=3,
        grid=(n_blocks,),
        in_specs=[pl.BlockSpec(memory_space=pl.ANY),
                  pl.BlockSpec((1, d, de), lambda i, be, nu, tok: (be[i], 0, 0)),
                  pl.BlockSpec((1, d, de), lambda i, be, nu, tok: (be[i], 0, 0)),
                  pl.BlockSpec((1, de, d), lambda i, be, nu, tok: (be[i], 0, 0))],
        out_specs=pl.BlockSpec((bm, d), lambda i, be, nu, tok: (jnp.minimum(i, nu[0] - 1), 0)),
        scratch_shapes=[pltpu.VMEM((2, bm, d), F32),
                        pltpu.SemaphoreType.DMA((2,)),
                        pltpu.VMEM((d, de), BF16),
                        pltpu.VMEM((d, de), BF16),
                        pltpu.VMEM((de, d), BF16)],
    )
    return pl.pallas_call(
        body,
        grid_spec=grid_spec,
        out_shape=jax.ShapeDtypeStruct((n_blocks * bm, d), F32),
        compiler_params=_params(1),
        name="experts",
    )(block_expert, n_used, row_tok, f, w_gate, w_up, w_down)


def _combine_body(d1_ref, d2_ref, yb_hbm, route_ref, h_ref, mod_ref, g_ref, o_ref, ybuf, sem):
    i = pl.program_id(0)
    n = pl.num_programs(0)
    tm = h_ref.shape[0]

    def start_gather(blk, slot):
        base = blk * tm

        def body(r, carry):
            pltpu.make_async_copy(yb_hbm.at[pl.ds(d1_ref[base + r], 1)], ybuf.at[slot, 0, pl.ds(r, 1)],
                                  sem.at[slot]).start()
            pltpu.make_async_copy(yb_hbm.at[pl.ds(d2_ref[base + r], 1)], ybuf.at[slot, 1, pl.ds(r, 1)],
                                  sem.at[slot]).start()
            return carry

        lax.fori_loop(0, tm, body, 0)

    def wait_gather(slot):
        def body(r, carry):
            for k in range(TOP_K):
                pltpu.make_async_copy(yb_hbm.at[pl.ds(0, 1)], ybuf.at[slot, k, pl.ds(r, 1)], sem.at[slot]).wait()
            return carry

        lax.fori_loop(0, tm, body, 0)

    @pl.when(i == 0)
    def _():
        start_gather(0, 0)

    @pl.when(i + 1 < n)
    def _():
        start_gather(i + 1, (i + 1) % 2)

    slot = i % 2
    wait_gather(slot)
    y = route_ref[:, 2:3] * ybuf[slot, 0] + route_ref[:, 3:4] * ybuf[slot, 1]
    o_ref[...] = h_ref[...] + mod_ref[0, 5:6, :] * _rms(y, g_ref[...])


def _combine(dest1, dest2, yb, route, h, mod3, gpost, *, tm, seq):
    rows, d = h.shape
    per_batch = seq // tm
    grid_spec = pltpu.PrefetchScalarGridSpec(
        num_scalar_prefetch=2,
        grid=(rows // tm,),
        in_specs=[pl.BlockSpec(memory_space=pl.ANY),
                  pl.BlockSpec((tm, LANES), lambda i, d1, d2: (i, 0)),
                  pl.BlockSpec((tm, d), lambda i, d1, d2: (i, 0)),
                  pl.BlockSpec((1, N_MOD, d), lambda i, d1, d2: (i // per_batch, 0, 0)),
                  pl.BlockSpec((1, d), lambda i, d1, d2: (0, 0))],
        out_specs=pl.BlockSpec((tm, d), lambda i, d1, d2: (i, 0)),
        scratch_shapes=[pltpu.VMEM((2, TOP_K, tm, d), F32),
                        pltpu.SemaphoreType.DMA((2,))],
    )
    return pl.pallas_call(
        _combine_body,
        grid_spec=grid_spec,
        out_shape=jax.ShapeDtypeStruct((rows, d), F32),
        compiler_params=_params(1),
        name="combine",
    )(dest1, dest2, yb, route, h, mod3, gpost)


def _rope_tables(seq):
    rows = seq // GRID_W
    row = jnp.repeat(jnp.arange(rows, dtype=F32), GRID_W)
    col = jnp.tile(jnp.arange(GRID_W, dtype=F32), rows)
    pairs = HEAD_DIM // 4
    freqs = ROPE_THETA ** (-jnp.arange(pairs, dtype=F32) / pairs)
    ang = jnp.concatenate([row[:, None] * freqs, col[:, None] * freqs], axis=-1)
    cos, sin = jnp.cos(ang), jnp.sin(ang)
    return jnp.concatenate([cos, cos], axis=-1), jnp.concatenate([-sin, sin], axis=-1)


def _tile(limit, size):
    t = min(limit, size)
    assert size % t == 0, (limit, size)
    return t


def kernel(x, c, ctx, c_ctx, w_ada, b_ada, g_pre_mix, w_in, q_norm, k_norm, conv_w, w_attn_out, w_conv_out,
           w_mix_out, g_post_mix, g_pre_ffn, w_router_group, b_router_group, w_router_expert, b_router_expert,
           w_gate, w_up, w_down, g_post_ffn):
    assert w_ada.shape[0] == 1, "single-layer problem"
    batch, seq, d = x.shape
    n_ctx = ctx.shape[1]
    attn_width = w_attn_out.shape[1]
    conv_width = conv_w.shape[-1]
    in_width = w_in.shape[-1]
    kv_width = (in_width - attn_width - 3 * conv_width - 2 * d) // 2
    n_kv = kv_width // HEAD_DIM
    group = attn_width // kv_width
    conv_start = attn_width + 2 * kv_width
    gate_start = conv_start + 3 * conv_width
    n_groups, per_group = w_router_expert.shape[2], w_router_expert.shape[3]
    n_experts = n_groups * per_group
    assert n_groups + n_experts <= LANES and batch + 1 <= 8
    n_tok = batch * seq

    cc = jnp.concatenate([c, c_ctx[None, :], jnp.zeros((8 - batch - 1, d), F32)], axis=0)
    mod3 = _adaln(cc, w_ada[0], b_ada[0]).reshape(8, N_MOD, d)

    w_in_b = w_in[0].astype(BF16)
    cos2, sin2 = _rope_tables(seq)
    tn = _tile(512, kv_width)
    tm_in = _tile(1024, seq)
    x2 = x.reshape(n_tok, d)
    common = dict(tn=tn, attn_width=attn_width, kv_width=kv_width, gate_start_col=gate_start)
    per_batch_in = seq // tm_in
    p_all = _inproj(x2, mod3, g_pre_mix, w_in_b, cos2, sin2, q_norm, k_norm, tm=tm_in, col_off=0,
                    n_col_tiles=in_width // tn, mod_row=lambda i: i // per_batch_in, pos_blocks=per_batch_in,
                    use_rope=True, **common)
    tm_ctx = _tile(1024, n_ctx)
    p_ctx = _inproj(ctx.reshape(batch * n_ctx, d), mod3, g_pre_mix, w_in_b, cos2[:tm_ctx], sin2[:tm_ctx],
                    q_norm, k_norm, tm=tm_ctx, col_off=attn_width // tn, n_col_tiles=2 * kv_width // tn,
                    mod_row=lambda i: batch, pos_blocks=1, use_rope=False, **common)

    kv_lat = p_all[:, attn_width:conv_start].reshape(batch, seq, 2 * kv_width)
    kv_all = jnp.concatenate([p_ctx.reshape(batch, n_ctx, 2 * kv_width), kv_lat], axis=1)
    k_all, v_all = kv_all[:, :, :kv_width], kv_all[:, :, kv_width:]
    attn = _attention(p_all, k_all, v_all, batch=batch, seq=seq, n_kv=n_kv, group=group,
                      tq=_tile(128, seq), tk=_tile(512, seq), n_ctx=n_ctx)

    merged = _merge(attn, p_all, conv_w[0], w_attn_out[0].astype(BF16), w_conv_out[0].astype(BF16),
                    tm=_tile(256, seq), seq=seq, conv_width=conv_width, d=d,
                    conv_start_col=conv_start, gate_start_col=gate_start)

    pad = LANES - n_groups - n_experts
    w_r = jnp.concatenate([w_router_group[0], w_router_expert[0].reshape(d, n_experts), jnp.zeros((d, pad), F32)], axis=1)
    b_r = jnp.concatenate([b_router_group[0], b_router_expert[0].reshape(n_experts), jnp.zeros((pad,), F32)])[None, :]
    tm_tok = _tile(256, seq)
    h, f, route, counts = _mix_route(merged, w_mix_out[0].astype(BF16), x2, mod3, g_post_mix, g_pre_ffn, w_r, b_r,
                                     tm=tm_tok, seq=seq, n_groups=n_groups, per_group=per_group)

    bm = _tile(256, n_tok)
    n_blocks = n_tok * TOP_K // bm + n_experts
    cnt = counts[0, :n_experts].astype(jnp.int32)
    padded = (cnt + bm - 1) // bm * bm
    pend = jnp.cumsum(padded)
    pstart = pend - padded
    e1, e2 = route[:, 0].astype(jnp.int32), route[:, 1].astype(jnp.int32)
    dest1 = pstart[e1] + route[:, 4].astype(jnp.int32)
    dest2 = pstart[e2] + route[:, 5].astype(jnp.int32)
    tok = jnp.arange(n_tok, dtype=jnp.int32)
    row_tok = jnp.zeros((n_blocks * bm,), jnp.int32).at[dest1].set(tok).at[dest2].set(tok)
    n_used = (pend[-1] // bm).astype(jnp.int32)
    blk = jnp.arange(n_blocks, dtype=jnp.int32)
    blk_exp = jnp.searchsorted(pend, jnp.minimum(blk, n_used - 1) * bm, side="right").astype(jnp.int32)
    blk_exp = jnp.minimum(blk_exp, n_experts - 1)

    yb = _experts(blk_exp, n_used.reshape(1), row_tok, f, w_gate[0], w_up[0], w_down[0], bm=bm)
    out = _combine(dest1, dest2, yb, route, h, mod3, g_post_ffn, tm=tm_tok, seq=seq)
    return out.reshape(batch, seq, d)
```

```python
import functools

import jax
import jax.numpy as jnp
from jax import lax
from jax.experimental import pallas as pl
from jax.experimental.pallas import tpu as pltpu

HEAD_DIM = 128
GRID_W = 64
ROPE_THETA = 10000.0
EPS = 1e-6
TOP_K = 2
N_MOD = 6
LANES = 128
BF16_SUBLANES = 16
VMEM_LIMIT_BYTES = 56 * 1024 * 1024
NEG_BIG = -1e30
LOG2_E = 1.4426950408889634

F32 = jnp.float32
BF16 = jnp.bfloat16


def _params(n_grid_axes):
    return pltpu.CompilerParams(dimension_semantics=("arbitrary",) * n_grid_axes,
                                vmem_limit_bytes=VMEM_LIMIT_BYTES)


def _rms(t, gain):
    ms = jnp.mean(t * t, axis=-1, keepdims=True)
    return t * lax.rsqrt(ms + EPS) * gain


def _adaln_body(c_ref, w_ref, b_ref, o_ref):
    c = c_ref[...]
    s = (c * jax.nn.sigmoid(c)).astype(BF16)
    o_ref[...] = jnp.dot(s, w_ref[...].astype(BF16), preferred_element_type=F32) + b_ref[...]


def _adaln(cc, w, b):
    rows, d = cc.shape
    n = w.shape[1]
    tn = min(1024, n)
    return pl.pallas_call(
        _adaln_body,
        grid=(n // tn,),
        in_specs=[pl.BlockSpec((rows, d), lambda j: (0, 0)),
                  pl.BlockSpec((d, tn), lambda j: (0, j)),
                  pl.BlockSpec((1, tn), lambda j: (0, j))],
        out_specs=pl.BlockSpec((rows, tn), lambda j: (0, j)),
        out_shape=jax.ShapeDtypeStruct((rows, n), F32),
        compiler_params=_params(1),
        name="adaln",
    )(cc, w, b.reshape(1, n))


def _inproj_body(x_ref, mod_ref, g_ref, w_ref, cos_ref, sin_ref, qn_ref, kn_ref, o_ref, a_scr, *,
                 col_off, n_q_tiles, n_k_tiles, gate_start, use_rope, q_scale):
    j = pl.program_id(1)

    @pl.when(j == 0)
    def _():
        xn = _rms(x_ref[...], g_ref[...])
        shift = mod_ref[0, 0:1, :]
        scale = mod_ref[0, 1:2, :]
        a_scr[...] = (xn * (1.0 + scale) + shift).astype(BF16)

    acc = jnp.dot(a_scr[...], w_ref[...], preferred_element_type=F32)
    jj = j + col_off
    heads = acc.shape[1] // HEAD_DIM

    def normed_heads(gain, out_scale):
        for hh in range(heads):
            sl = slice(hh * HEAD_DIM, (hh + 1) * HEAD_DIM)
            t = _rms(acc[:, sl], gain)
            if use_rope:
                t = t * cos_ref[...] + pltpu.roll(t, HEAD_DIM // 2, 1) * sin_ref[...]
            if out_scale != 1.0:
                t = t * out_scale
            o_ref[:, sl] = t.astype(o_ref.dtype)

    @pl.when(jj < n_q_tiles)
    def _():
        normed_heads(qn_ref[...], q_scale)

    @pl.when((jj >= n_q_tiles) & (jj < n_q_tiles + n_k_tiles))
    def _():
        normed_heads(kn_ref[...], 1.0)

    @pl.when((jj >= n_q_tiles + n_k_tiles) & (jj < gate_start))
    def _():
        o_ref[...] = acc.astype(o_ref.dtype)

    @pl.when(jj >= gate_start)
    def _():
        o_ref[...] = jax.nn.sigmoid(acc).astype(o_ref.dtype)


def _inproj(x2, mod3, g, w_b, cos2, sin2, qn, kn, *, tm, tn, col_off, n_col_tiles, mod_row, pos_blocks,
            attn_width, kv_width, gate_start_col, use_rope):
    rows, d = x2.shape
    body = functools.partial(
        _inproj_body, col_off=col_off, n_q_tiles=attn_width // tn, n_k_tiles=kv_width // tn,
        gate_start=gate_start_col // tn, use_rope=use_rope, q_scale=HEAD_DIM ** -0.5 * LOG2_E)
    return pl.pallas_call(
        body,
        grid=(rows // tm, n_col_tiles),
        in_specs=[pl.BlockSpec((tm, d), lambda i, j: (i, 0)),
                  pl.BlockSpec((1, N_MOD, d), lambda i, j: (mod_row(i), 0, 0)),
                  pl.BlockSpec((1, d), lambda i, j: (0, 0)),
                  pl.BlockSpec((d, tn), lambda i, j: (0, j + col_off)),
                  pl.BlockSpec((tm, HEAD_DIM), lambda i, j: (i % pos_blocks, 0)),
                  pl.BlockSpec((tm, HEAD_DIM), lambda i, j: (i % pos_blocks, 0)),
                  pl.BlockSpec((1, HEAD_DIM), lambda i, j: (0, 0)),
                  pl.BlockSpec((1, HEAD_DIM), lambda i, j: (0, 0))],
        out_specs=pl.BlockSpec((tm, tn), lambda i, j: (i, j)),
        out_shape=jax.ShapeDtypeStruct((rows, n_col_tiles * tn), BF16),
        scratch_shapes=[pltpu.VMEM((tm, d), BF16)],
        compiler_params=_params(2),
        name="inproj_rope" if use_rope else "inproj_ctx",
    )(x2, mod3, g, w_b, cos2, sin2, qn, kn)


def _attn_body(q_ref, k_ref, vt_ref, o_ref, *, n_ctx, tk, group):
    tq = q_ref.shape[0]
    n_keys = k_ref.shape[1]
    q = jnp.concatenate([q_ref[:, g * HEAD_DIM:(g + 1) * HEAD_DIM] for g in range(group)], axis=0)
    cols = q.shape[0]

    s = lax.dot_general(k_ref[0], q, (((1,), (1,)), ((), ())), preferred_element_type=F32)
    m = jnp.max(s, axis=0, keepdims=True)
    p = jnp.exp2(s - m)
    l = jnp.sum(p, axis=0, keepdims=True)
    acc = jnp.dot(vt_ref[0, 0], p.astype(BF16), preferred_element_type=F32)
    out = acc * (1.0 / l)
    for g in range(group):
        o_ref[:, g * HEAD_DIM:(g + 1) * HEAD_DIM] = out[:, g * tq:(g + 1) * tq].T.astype(o_ref.dtype)


def _attention(p_all, k_all, vt_all, *, batch, seq, n_kv, group, tq, tk, n_ctx):
    gw = group * HEAD_DIM
    q_tiles = seq // tq
    body = functools.partial(_attn_body, n_ctx=n_ctx, tk=tk, group=group)
    n_keys = k_all.shape[1]
    return pl.pallas_call(
        body,
        grid=(batch, n_kv, q_tiles),
        in_specs=[pl.BlockSpec((tq, gw), lambda b, h, i: (b * q_tiles + i, h)),
                  pl.BlockSpec((1, n_keys, HEAD_DIM), lambda b, h, i: (b, 0, h)),
                  pl.BlockSpec((1, 1, HEAD_DIM, n_keys), lambda b, h, i: (b, h, 0, 0))],
        out_specs=pl.BlockSpec((tq, gw), lambda b, h, i: (b * q_tiles + i, h)),
        out_shape=jax.ShapeDtypeStruct((batch * seq, n_kv * gw), BF16),
        compiler_params=_params(3),
        name="attention",
    )(p_all, k_all, vt_all)


def _merge_body(attn_ref, pb_ref, pc_ref, px_ref, pcp_ref, pxp_ref, pcn_ref, pxn_ref, ga_ref, gc_ref,
                cw_ref, wa_ref, wc_ref, o_ref, *, tiles_per_seq):
    i = pl.program_id(0)
    tm = attn_ref.shape[0]
    u = pc_ref[...].astype(F32) * px_ref[...].astype(F32)
    last = BF16_SUBLANES - 1
    u_before = pcp_ref[last:last + 1, :].astype(F32) * pxp_ref[last:last + 1, :].astype(F32)
    u_after = pcn_ref[0:1, :].astype(F32) * pxn_ref[0:1, :].astype(F32)
    pos = i % tiles_per_seq
    u_before = jnp.where(pos == 0, 0.0, u_before)
    u_after = jnp.where(pos == tiles_per_seq - 1, 0.0, u_after)
    row = lax.broadcasted_iota(jnp.int32, u.shape, 0)
    u_prev = jnp.where(row == 0, u_before, pltpu.roll(u, 1, 0))
    u_next = jnp.where(row == tm - 1, u_after, pltpu.roll(u, tm - 1, 0))
    conv = cw_ref[0:1, :] * u_prev + cw_ref[1:2, :] * u + cw_ref[2:3, :] * u_next
    z = (pb_ref[...].astype(F32) * conv).astype(BF16)
    y_conv = jnp.dot(z, wc_ref[...], preferred_element_type=F32)
    y_attn = jnp.dot(attn_ref[...], wa_ref[...], preferred_element_type=F32)
    merged = ga_ref[...].astype(F32) * y_attn + gc_ref[...].astype(F32) * y_conv
    o_ref[...] = merged.astype(o_ref.dtype)


def _merge(attn, p_all, conv_w, wa_b, wc_b, *, tm, seq, conv_width, d, conv_start_col, gate_start_col):
    rows, attn_width = attn.shape
    hb = BF16_SUBLANES
    cb = conv_start_col // conv_width
    gb = gate_start_col // d
    n_hblocks = rows // hb
    per = tm // hb

    def prev_map(col):
        return lambda i: (jnp.maximum(i * per - 1, 0), col)

    def next_map(col):
        return lambda i: (jnp.minimum((i + 1) * per, n_hblocks - 1), col)

    body = functools.partial(_merge_body, tiles_per_seq=seq // tm)
    resident = dict(pipeline_mode=pl.Buffered(1))
    return pl.pallas_call(
        body,
        grid=(rows // tm,),
        in_specs=[pl.BlockSpec((tm, attn_width), lambda i: (i, 0)),
                  pl.BlockSpec((tm, conv_width), lambda i: (i, cb)),
                  pl.BlockSpec((tm, conv_width), lambda i: (i, cb + 1)),
                  pl.BlockSpec((tm, conv_width), lambda i: (i, cb + 2)),
                  pl.BlockSpec((hb, conv_width), prev_map(cb + 1)),
                  pl.BlockSpec((hb, conv_width), prev_map(cb + 2)),
                  pl.BlockSpec((hb, conv_width), next_map(cb + 1)),
                  pl.BlockSpec((hb, conv_width), next_map(cb + 2)),
                  pl.BlockSpec((tm, d), lambda i: (i, gb)),
                  pl.BlockSpec((tm, d), lambda i: (i, gb + 1)),
                  pl.BlockSpec(conv_w.shape, lambda i: (0, 0)),
                  pl.BlockSpec(wa_b.shape, lambda i: (0, 0), **resident),
                  pl.BlockSpec(wc_b.shape, lambda i: (0, 0), **resident)],
        out_specs=pl.BlockSpec((tm, d), lambda i: (i, 0)),
        out_shape=jax.ShapeDtypeStruct((rows, d), BF16),
        compiler_params=_params(1),
        name="merge",
    )(attn, p_all, p_all, p_all, p_all, p_all, p_all, p_all, p_all, p_all, conv_w, wa_b, wc_b)


def _split_bf16(t):
    hi = t.astype(BF16)
    lo = (t - hi.astype(F32)).astype(BF16)
    return hi, lo


def _mix_route_body(m_ref, wm_ref, x_ref, mod_ref, gpost_ref, gpre_ref, wr_ref, br_ref,
                    h_ref, f_ref, route_ref, cnt_ref, carry_scr, *, n_groups, per_group):
    i = pl.program_id(0)
    tm = m_ref.shape[0]

    @pl.when(i == 0)
    def _():
        carry_scr[...] = jnp.zeros_like(carry_scr)

    y = jnp.dot(m_ref[...], wm_ref[...], preferred_element_type=F32)
    h = x_ref[...] + mod_ref[0, 2:3, :] * _rms(y, gpost_ref[...])
    h_ref[...] = h
    f = _rms(h, gpre_ref[...]) * (1.0 + mod_ref[0, 4:5, :]) + mod_ref[0, 3:4, :]
    f_ref[...] = f

    f_hi, f_lo = _split_bf16(f)
    w_hi, w_lo = _split_bf16(wr_ref[...])
    logits = (jnp.dot(f_hi, w_hi, preferred_element_type=F32)
              + jnp.dot(f_hi, w_lo, preferred_element_type=F32)
              + jnp.dot(f_lo, w_hi, preferred_element_type=F32)) + br_ref[...]

    lane = lax.broadcasted_iota(jnp.int32, logits.shape, 1)

    def first_argmax(vals):
        mx = jnp.max(vals, axis=-1, keepdims=True)
        idx = jnp.min(jnp.where(vals == mx, lane, LANES), axis=-1, keepdims=True)
        return mx, idx

    is_group = lane < n_groups
    g_max, g_idx = first_argmax(jnp.where(is_group, logits, NEG_BIG))
    g_den = jnp.sum(jnp.where(is_group, jnp.exp(logits - g_max), 0.0), axis=-1, keepdims=True)
    grp_p = 1.0 / g_den
    lo_lane = n_groups + g_idx * per_group
    e_logits = jnp.where((lane >= lo_lane) & (lane < lo_lane + per_group), logits, NEG_BIG)
    l1, i1 = first_argmax(e_logits)
    l2, i2 = first_argmax(jnp.where(lane == i1, NEG_BIG, e_logits))
    r = jnp.exp(l2 - l1)
    gate1 = grp_p / (1.0 + r)
    gate2 = grp_p * r / (1.0 + r)
    e1 = i1 - n_groups
    e2 = i2 - n_groups

    onehot = ((lane == e1) | (lane == e2)).astype(BF16)
    r_i = lax.broadcasted_iota(jnp.int32, (tm, tm), 0)
    c_i = lax.broadcasted_iota(jnp.int32, (tm, tm), 1)
    before = (c_i < r_i).astype(BF16)
    seen = jnp.dot(before, onehot, preferred_element_type=F32) + carry_scr[0:1, :]
    rank1 = jnp.sum(jnp.where(lane == e1, seen, 0.0), axis=-1, keepdims=True)
    rank2 = jnp.sum(jnp.where(lane == e2, seen, 0.0), axis=-1, keepdims=True)
    carry_scr[...] = carry_scr[...] + jnp.sum(onehot.astype(F32), axis=0, keepdims=True)
    cnt_ref[...] = carry_scr[...]

    route = jnp.where(lane == 0, e1.astype(F32), 0.0)
    route = jnp.where(lane == 1, e2.astype(F32), route)
    route = jnp.where(lane == 2, gate1, route)
    route = jnp.where(lane == 3, gate2, route)
    route = jnp.where(lane == 4, rank1, route)
    route = jnp.where(lane == 5, rank2, route)
    route_ref[...] = route


def _mix_route(merged, wm_b, x2, mod3, gpost, gpre, w_r, b_r, *, tm, seq, n_groups, per_group):
    rows, d = x2.shape
    per_batch = seq // tm
    body = functools.partial(_mix_route_body, n_groups=n_groups, per_group=per_group)
    return pl.pallas_call(
        body,
        grid=(rows // tm,),
        in_specs=[pl.BlockSpec((tm, d), lambda i: (i, 0)),
                  pl.BlockSpec(wm_b.shape, lambda i: (0, 0), pipeline_mode=pl.Buffered(1)),
                  pl.BlockSpec((tm, d), lambda i: (i, 0)),
                  pl.BlockSpec((1, N_MOD, d), lambda i: (i // per_batch, 0, 0)),
                  pl.BlockSpec((1, d), lambda i: (0, 0)),
                  pl.BlockSpec((1, d), lambda i: (0, 0)),
                  pl.BlockSpec((d, LANES), lambda i: (0, 0)),
                  pl.BlockSpec((1, LANES), lambda i: (0, 0))],
        out_specs=[pl.BlockSpec((tm, d), lambda i: (i, 0)),
                   pl.BlockSpec((tm, d), lambda i: (i, 0)),
                   pl.BlockSpec((tm, LANES), lambda i: (i, 0)),
                   pl.BlockSpec((8, LANES), lambda i: (0, 0))],
        out_shape=[jax.ShapeDtypeStruct((rows, d), F32),
                   jax.ShapeDtypeStruct((rows, d), F32),
                   jax.ShapeDtypeStruct((rows, LANES), F32),
                   jax.ShapeDtypeStruct((8, LANES), F32)],
        scratch_shapes=[pltpu.VMEM((8, LANES), F32)],
        compiler_params=_params(1),
        name="mix_route",
    )(merged, wm_b, x2, mod3, gpost, gpre, w_r, b_r)


def _experts_body(be_ref, nu_ref, tok_ref, f_hbm, wg_ref, wu_ref, wd_ref, o_ref,
                  xbuf, sem, wg_b, wu_b, wd_b, *, bm):
    i = pl.program_id(0)
    n_used = nu_ref[0]

    def start_gather(blk, slot):
        base = blk * bm

        def body(r, carry):
            tok = tok_ref[base + r]
            pltpu.make_async_copy(f_hbm.at[pl.ds(tok, 1)], xbuf.at[slot, pl.ds(r, 1)], sem.at[slot]).start()
            return carry

        lax.fori_loop(0, bm, body, 0)

    def wait_gather(slot):
        def body(r, carry):
            pltpu.make_async_copy(f_hbm.at[pl.ds(0, 1)], xbuf.at[slot, pl.ds(r, 1)], sem.at[slot]).wait()
            return carry

        lax.fori_loop(0, bm, body, 0)

    @pl.when(i == 0)
    def _():
        start_gather(0, 0)

    @pl.when(i + 1 < n_used)
    def _():
        start_gather(i + 1, (i + 1) % 2)

    @pl.when(i < n_used)
    def _():
        slot = i % 2
        changed = (i == 0) | (be_ref[i] != be_ref[jnp.maximum(i - 1, 0)])

        @pl.when(changed)
        def _():
            wg_b[...] = wg_ref[0].astype(BF16)
            wu_b[...] = wu_ref[0].astype(BF16)
            wd_b[...] = wd_ref[0].astype(BF16)

        wait_gather(slot)
        x = xbuf[slot].astype(BF16)
        g = jnp.dot(x, wg_b[...], preferred_element_type=F32)
        u = jnp.dot(x, wu_b[...], preferred_element_type=F32)
        hid = (g * jax.nn.sigmoid(g) * u).astype(BF16)
        o_ref[...] = jnp.dot(hid, wd_b[...], preferred_element_type=F32)


def _experts(block_expert, n_used, row_tok, f, w_gate, w_up, w_down, *, bm):
    n_blocks = block_expert.shape[0]
    _, d, de = w_gate.shape
    body = functools.partial(_experts_body, bm=bm)
    grid_spec = pltpu.PrefetchScalarGridSpec(
        num_scalar_prefetch=3,
        grid=(n_blocks,),
        in_specs=[pl.BlockSpec(memory_space=pl.ANY),
                  pl.BlockSpec((1, d, de), lambda i, be, nu, tok: (be[i], 0, 0)),
                  pl.BlockSpec((1, d, de), lambda i, be, nu, tok: (be[i], 0, 0)),
                  pl.BlockSpec((1, de, d), lambda i, be, nu, tok: (be[i], 0, 0))],
        out_specs=pl.BlockSpec((bm, d), lambda i, be, nu, tok: (jnp.minimum(i, nu[0] - 1), 0)),
        scratch_shapes=[pltpu.VMEM((2, bm, d), F32),
                        pltpu.SemaphoreType.DMA((2,)),
                        pltpu.VMEM((d, de), BF16),
                        pltpu.VMEM((d, de), BF16),
                        pltpu.VMEM((de, d), BF16)],
    )
    return pl.pallas_call(
        body,
        grid_spec=grid_spec,
        out_shape=jax.ShapeDtypeStruct((n_blocks * bm, d), F32),
        compiler_params=_params(1),
        name="experts",
    )(block_expert, n_used, row_tok, f, w_gate, w_up, w_down)


def _combine_body(d1_ref, d2_ref, yb_hbm, route_ref, h_ref, mod_ref, g_ref, o_ref, ybuf, sem):
    i = pl.program_id(0)
    n = pl.num_programs(0)
    tm = h_ref.shape[0]

    def start_gather(blk, slot):
        base = blk * tm

        def body(r, carry):
            pltpu.make_async_copy(yb_hbm.at[pl.ds(d1_ref[base + r], 1)], ybuf.at[slot, 0, pl.ds(r, 1)],
                                  sem.at[slot]).start()
            pltpu.make_async_copy(yb_hbm.at[pl.ds(d2_ref[base + r], 1)], ybuf.at[slot, 1, pl.ds(r, 1)],
                                  sem.at[slot]).start()
            return carry

        lax.fori_loop(0, tm, body, 0)

    def wait_gather(slot):
        def body(r, carry):
            for k in range(TOP_K):
                pltpu.make_async_copy(yb_hbm.at[pl.ds(0, 1)], ybuf.at[slot, k, pl.ds(r, 1)], sem.at[slot]).wait()
            return carry

        lax.fori_loop(0, tm, body, 0)

    @pl.when(i == 0)
    def _():
        start_gather(0, 0)

    @pl.when(i + 1 < n)
    def _():
        start_gather(i + 1, (i + 1) % 2)

    slot = i % 2
    wait_gather(slot)
    y = route_ref[:, 2:3] * ybuf[slot, 0] + route_ref[:, 3:4] * ybuf[slot, 1]
    o_ref[...] = h_ref[...] + mod_ref[0, 5:6, :] * _rms(y, g_ref[...])


def _combine(dest1, dest2, yb, route, h, mod3, gpost, *, tm, seq):
    rows, d = h.shape
    per_batch = seq // tm
    grid_spec = pltpu.PrefetchScalarGridSpec(
        num_scalar_prefetch=2,
        grid=(rows // tm,),
        in_specs=[pl.BlockSpec(memory_space=pl.ANY),
                  pl.BlockSpec((tm, LANES), lambda i, d1, d2: (i, 0)),
                  pl.BlockSpec((tm, d), lambda i, d1, d2: (i, 0)),
                  pl.BlockSpec((1, N_MOD, d), lambda i, d1, d2: (i // per_batch, 0, 0)),
                  pl.BlockSpec((1, d), lambda i, d1, d2: (0, 0))],
        out_specs=pl.BlockSpec((tm, d), lambda i, d1, d2: (i, 0)),
        scratch_shapes=[pltpu.VMEM((2, TOP_K, tm, d), F32),
                        pltpu.SemaphoreType.DMA((2,))],
    )
    return pl.pallas_call(
        _combine_body,
        grid_spec=grid_spec,
        out_shape=jax.ShapeDtypeStruct((rows, d), F32),
        compiler_params=_params(1),
        name="combine",
    )(dest1, dest2, yb, route, h, mod3, gpost)


def _rope_tables(seq):
    rows = seq // GRID_W
    row = jnp.repeat(jnp.arange(rows, dtype=F32), GRID_W)
    col = jnp.tile(jnp.arange(GRID_W, dtype=F32), rows)
    pairs = HEAD_DIM // 4
    freqs = ROPE_THETA ** (-jnp.arange(pairs, dtype=F32) / pairs)
    ang = jnp.concatenate([row[:, None] * freqs, col[:, None] * freqs], axis=-1)
    cos, sin = jnp.cos(ang), jnp.sin(ang)
    return jnp.concatenate([cos, cos], axis=-1), jnp.concatenate([-sin, sin], axis=-1)


def _tile(limit, size):
    t = min(limit, size)
    assert size % t == 0, (limit, size)
    return t


def kernel(x, c, ctx, c_ctx, w_ada, b_ada, g_pre_mix, w_in, q_norm, k_norm, conv_w, w_attn_out, w_conv_out,
           w_mix_out, g_post_mix, g_pre_ffn, w_router_group, b_router_group, w_router_expert, b_router_expert,
           w_gate, w_up, w_down, g_post_ffn):
    assert w_ada.shape[0] == 1, "single-layer problem"
    batch, seq, d = x.shape
    n_ctx = ctx.shape[1]
    attn_width = w_attn_out.shape[1]
    conv_width = conv_w.shape[-1]
    in_width = w_in.shape[-1]
    kv_width = (in_width - attn_width - 3 * conv_width - 2 * d) // 2
    n_kv = kv_width // HEAD_DIM
    group = attn_width // kv_width
    conv_start = attn_width + 2 * kv_width
    gate_start = conv_start + 3 * conv_width
    n_groups, per_group = w_router_expert.shape[2], w_router_expert.shape[3]
    n_experts = n_groups * per_group
    assert n_groups + n_experts <= LANES and batch + 1 <= 8
    n_tok = batch * seq

    cc = jnp.concatenate([c, c_ctx[None, :], jnp.zeros((8 - batch - 1, d), F32)], axis=0)
    mod3 = _adaln(cc, w_ada[0], b_ada[0]).reshape(8, N_MOD, d)

    w_in_b = w_in[0].astype(BF16)
    cos2, sin2 = _rope_tables(seq)
    tn = _tile(512, kv_width)
    tm_in = _tile(1024, seq)
    x2 = x.reshape(n_tok, d)
    common = dict(tn=tn, attn_width=attn_width, kv_width=kv_width, gate_start_col=gate_start)
    per_batch_in = seq // tm_in
    p_all = _inproj(x2, mod3, g_pre_mix, w_in_b, cos2, sin2, q_norm, k_norm, tm=tm_in, col_off=0,
                    n_col_tiles=in_width // tn, mod_row=lambda i: i // per_batch_in, pos_blocks=per_batch_in,
                    use_rope=True, **common)
    tm_ctx = _tile(1024, n_ctx)
    p_ctx = _inproj(ctx.reshape(batch * n_ctx, d), mod3, g_pre_mix, w_in_b, cos2[:tm_ctx], sin2[:tm_ctx],
                    q_norm, k_norm, tm=tm_ctx, col_off=attn_width // tn, n_col_tiles=2 * kv_width // tn,
                    mod_row=lambda i: batch, pos_blocks=1, use_rope=False, **common)

    kv_lat = p_all[:, attn_width:conv_start].reshape(batch, seq, 2 * kv_width)
    kv_all = jnp.concatenate([p_ctx.reshape(batch, n_ctx, 2 * kv_width), kv_lat], axis=1)
    k_all = kv_all[:, :, :kv_width]
    vt_all = kv_all[:, :, kv_width:].reshape(batch, n_ctx + seq, n_kv, HEAD_DIM).transpose(0, 2, 3, 1)
    attn = _attention(p_all, k_all, vt_all, batch=batch, seq=seq, n_kv=n_kv, group=group,
                      tq=_tile(128, seq), tk=_tile(512, seq), n_ctx=n_ctx)

    merged = _merge(attn, p_all, conv_w[0], w_attn_out[0].astype(BF16), w_conv_out[0].astype(BF16),
                    tm=_tile(256, seq), seq=seq, conv_width=conv_width, d=d,
                    conv_start_col=conv_start, gate_start_col=gate_start)

    pad = LANES - n_groups - n_experts
    w_r = jnp.concatenate([w_router_group[0], w_router_expert[0].reshape(d, n_experts), jnp.zeros((d, pad), F32)], axis=1)
    b_r = jnp.concatenate([b_router_group[0], b_router_expert[0].reshape(n_experts), jnp.zeros((pad,), F32)])[None, :]
    tm_tok = _tile(256, seq)
    h, f, route, counts = _mix_route(merged, w_mix_out[0].astype(BF16), x2, mod3, g_post_mix, g_pre_ffn, w_r, b_r,
                                     tm=tm_tok, seq=seq, n_groups=n_groups, per_group=per_group)

    bm = _tile(256, n_tok)
    n_blocks = n_tok * TOP_K // bm + n_experts
    cnt = counts[0, :n_experts].astype(jnp.int32)
    padded = (cnt + bm - 1) // bm * bm
    pend = jnp.cumsum(padded)
    pstart = pend - padded
    e1, e2 = route[:, 0].astype(jnp.int32), route[:, 1].astype(jnp.int32)
    expert_ids = jnp.arange(n_experts, dtype=jnp.int32)[None, :]

    def segment_start(e):
        return jnp.sum(jnp.where(e[:, None] == expert_ids, pstart[None, :], 0), axis=1)

    dest1 = segment_start(e1) + route[:, 4].astype(jnp.int32)
    dest2 = segment_start(e2) + route[:, 5].astype(jnp.int32)
    tok = jnp.arange(n_tok, dtype=jnp.int32)
    row_tok = jnp.zeros((n_blocks * bm,), jnp.int32).at[dest1].set(tok).at[dest2].set(tok)
    n_used = (pend[-1] // bm).astype(jnp.int32)
    blk_row = jnp.minimum(jnp.arange(n_blocks, dtype=jnp.int32), n_used - 1) * bm
    blk_exp = jnp.sum((pend[None, :] <= blk_row[:, None]).astype(jnp.int32), axis=1)
    blk_exp = jnp.minimum(blk_exp, n_experts - 1)

    yb = _experts(blk_exp, n_used.reshape(1), row_tok, f, w_gate[0], w_up[0], w_down[0], bm=bm)
    out = _combine(dest1, dest2, yb, route, h, mod3, g_post_ffn, tm=tm_tok, seq=seq)
    return out.reshape(batch, seq, d)
```

```python
import functools

import jax
import jax.numpy as jnp
from jax import lax
from jax.experimental import pallas as pl
from jax.experimental.pallas import tpu as pltpu

HEAD_DIM = 128
GRID_W = 64
ROPE_THETA = 10000.0
EPS = 1e-6
TOP_K = 2
N_MOD = 6
LANES = 128
BF16_SUBLANES = 16
VMEM_LIMIT_BYTES = 56 * 1024 * 1024
NEG_BIG = -1e30
LOG2_E = 1.4426950408889634

F32 = jnp.float32
BF16 = jnp.bfloat16


def _params(n_grid_axes):
    return pltpu.CompilerParams(dimension_semantics=("arbitrary",) * n_grid_axes,
                                vmem_limit_bytes=VMEM_LIMIT_BYTES)


def _rms(t, gain):
    ms = jnp.mean(t * t, axis=-1, keepdims=True)
    return t * lax.rsqrt(ms + EPS) * gain


def _adaln_body(c_ref, w_ref, b_ref, o_ref):
    c = c_ref[...]
    s = (c * jax.nn.sigmoid(c)).astype(BF16)
    o_ref[...] = jnp.dot(s, w_ref[...].astype(BF16), preferred_element_type=F32) + b_ref[...]


def _adaln(cc, w, b):
    rows, d = cc.shape
    n = w.shape[1]
    tn = min(1024, n)
    return pl.pallas_call(
        _adaln_body,
        grid=(n // tn,),
        in_specs=[pl.BlockSpec((rows, d), lambda j: (0, 0)),
                  pl.BlockSpec((d, tn), lambda j: (0, j)),
                  pl.BlockSpec((1, tn), lambda j: (0, j))],
        out_specs=pl.BlockSpec((rows, tn), lambda j: (0, j)),
        out_shape=jax.ShapeDtypeStruct((rows, n), F32),
        compiler_params=_params(1),
        name="adaln",
    )(cc, w, b.reshape(1, n))


Q_COLS, K_COLS, PLAIN_COLS, GATE_COLS = "q", "k", "plain", "gate"


def _inproj_body(x_ref, mod_ref, g_ref, w_ref, cos_ref, sin_ref, qn_ref, kn_ref, o_ref, *,
                 sections, tn, use_rope, q_scale):
    sec = pl.program_id(0)
    xn = _rms(x_ref[...], g_ref[...])
    a = (xn * (1.0 + mod_ref[0, 1:2, :]) + mod_ref[0, 0:1, :]).astype(BF16)

    def normed_heads(acc, col0, gain, out_scale):
        for hh in range(tn // HEAD_DIM):
            t = _rms(acc[:, hh * HEAD_DIM:(hh + 1) * HEAD_DIM], gain)
            if use_rope:
                t = t * cos_ref[...] + pltpu.roll(t, HEAD_DIM // 2, 1) * sin_ref[...]
            if out_scale != 1.0:
                t = t * out_scale
            o_ref[:, col0 + hh * HEAD_DIM:col0 + (hh + 1) * HEAD_DIM] = t.astype(o_ref.dtype)

    def section(kinds):
        for t, kind in enumerate(kinds):
            col0 = t * tn
            acc = jnp.dot(a, w_ref[:, col0:col0 + tn], preferred_element_type=F32)
            if kind == Q_COLS:
                normed_heads(acc, col0, qn_ref[...], q_scale)
            elif kind == K_COLS:
                normed_heads(acc, col0, kn_ref[...], 1.0)
            elif kind == PLAIN_COLS:
                o_ref[:, col0:col0 + tn] = acc.astype(o_ref.dtype)
            else:
                o_ref[:, col0:col0 + tn] = jax.nn.sigmoid(acc).astype(o_ref.dtype)

    for s_idx, kinds in enumerate(sections):
        pl.when(sec == s_idx)(functools.partial(section, kinds))


def _inproj(x2, mod3, g, w_b, cos2, sin2, qn, kn, *, tm, tn, sec_width, sec_off, n_sections, mod_row, pos_blocks,
            attn_width, kv_width, gate_start_col, use_rope):
    rows, d = x2.shape

    def kind_of(col):
        if col < attn_width:
            return Q_COLS
        if col < attn_width + kv_width:
            return K_COLS
        return PLAIN_COLS if col < gate_start_col else GATE_COLS

    sections = tuple(tuple(kind_of((sec_off + s) * sec_width + t * tn) for t in range(sec_width // tn))
                     for s in range(n_sections))
    body = functools.partial(_inproj_body, sections=sections, tn=tn, use_rope=use_rope,
                             q_scale=HEAD_DIM ** -0.5 * LOG2_E)
    return pl.pallas_call(
        body,
        grid=(n_sections, rows // tm),
        in_specs=[pl.BlockSpec((tm, d), lambda s, i: (i, 0)),
                  pl.BlockSpec((1, N_MOD, d), lambda s, i: (mod_row(i), 0, 0)),
                  pl.BlockSpec((1, d), lambda s, i: (0, 0)),
                  pl.BlockSpec((d, sec_width), lambda s, i: (0, s + sec_off), pipeline_mode=pl.Buffered(1)),
                  pl.BlockSpec((tm, HEAD_DIM), lambda s, i: (i % pos_blocks, 0)),
                  pl.BlockSpec((tm, HEAD_DIM), lambda s, i: (i % pos_blocks, 0)),
                  pl.BlockSpec((1, HEAD_DIM), lambda s, i: (0, 0)),
                  pl.BlockSpec((1, HEAD_DIM), lambda s, i: (0, 0))],
        out_specs=pl.BlockSpec((tm, sec_width), lambda s, i: (i, s)),
        out_shape=jax.ShapeDtypeStruct((rows, n_sections * sec_width), BF16),
        compiler_params=_params(2),
        name="inproj_rope" if use_rope else "inproj_ctx",
    )(x2, mod3, g, w_b, cos2, sin2, qn, kn)


def _attn_body(q_ref, k_ref, vt_ref, o_ref, s0_scr, m0_scr, s1_scr, m1_scr, *, group, q_tiles):
    i = pl.program_id(2)
    tq = q_ref.shape[0]

    slabs = ((s0_scr, m0_scr), (s1_scr, m1_scr))

    def scores(slab):
        s_scr, m_scr = slab
        q = jnp.concatenate([q_ref[:, g * HEAD_DIM:(g + 1) * HEAD_DIM] for g in range(group)], axis=0)
        s = lax.dot_general(k_ref[0], q, (((1,), (1,)), ((), ())), preferred_element_type=F32)
        s_scr[...] = s
        m_scr[...] = jnp.max(s, axis=0, keepdims=True)

    def finish(slab):
        s_scr, m_scr = slab
        p = jnp.exp2(s_scr[...] - m_scr[...])
        l = jnp.sum(p, axis=0, keepdims=True)
        acc = jnp.dot(vt_ref[0, 0], p.astype(BF16), preferred_element_type=F32)
        out = acc * (1.0 / l)
        for g in range(group):
            o_ref[:, g * HEAD_DIM:(g + 1) * HEAD_DIM] = out[:, g * tq:(g + 1) * tq].T.astype(o_ref.dtype)

    for parity in range(2):
        @pl.when((i % 2 == parity) & (i < q_tiles) & (i > 0))
        def _(parity=parity):
            scores(slabs[parity])
            finish(slabs[1 - parity])

    @pl.when(i == 0)
    def _():
        scores(slabs[0])

    @pl.when(i == q_tiles)
    def _():
        finish(slabs[(q_tiles - 1) % 2])


def _attention(p_all, k_all, vt_all, *, batch, seq, n_kv, group, tq):
    gw = group * HEAD_DIM
    q_tiles = seq // tq
    body = functools.partial(_attn_body, group=group, q_tiles=q_tiles)
    n_keys = k_all.shape[1]
    return pl.pallas_call(
        body,
        grid=(batch, n_kv, q_tiles + 1),
        in_specs=[pl.BlockSpec((tq, gw), lambda b, h, i: (b * q_tiles + jnp.minimum(i, q_tiles - 1), h)),
                  pl.BlockSpec((1, n_keys, HEAD_DIM), lambda b, h, i: (b, 0, h)),
                  pl.BlockSpec((1, 1, HEAD_DIM, n_keys), lambda b, h, i: (b, h, 0, 0))],
        out_specs=pl.BlockSpec((tq, gw), lambda b, h, i: (b * q_tiles + jnp.maximum(i - 1, 0), h)),
        out_shape=jax.ShapeDtypeStruct((batch * seq, n_kv * gw), BF16),
        scratch_shapes=[pltpu.VMEM((n_keys, group * tq), F32), pltpu.VMEM((1, group * tq), F32)] * 2,
        compiler_params=_params(3),
        name="attention",
    )(p_all, k_all, vt_all)


def _merge_body(attn_ref, pb_ref, pc_ref, px_ref, pcp_ref, pxp_ref, pcn_ref, pxn_ref, ga_ref, gc_ref,
                cw_ref, wa_ref, wc_ref, o_ref, *, tiles_per_seq):
    i = pl.program_id(0)
    tm = attn_ref.shape[0]
    u = pc_ref[...].astype(F32) * px_ref[...].astype(F32)
    last = BF16_SUBLANES - 1
    u_before = pcp_ref[last:last + 1, :].astype(F32) * pxp_ref[last:last + 1, :].astype(F32)
    u_after = pcn_ref[0:1, :].astype(F32) * pxn_ref[0:1, :].astype(F32)
    pos = i % tiles_per_seq
    u_before = jnp.where(pos == 0, 0.0, u_before)
    u_after = jnp.where(pos == tiles_per_seq - 1, 0.0, u_after)
    row = lax.broadcasted_iota(jnp.int32, u.shape, 0)
    u_prev = jnp.where(row == 0, u_before, pltpu.roll(u, 1, 0))
    u_next = jnp.where(row == tm - 1, u_after, pltpu.roll(u, tm - 1, 0))
    conv = cw_ref[0:1, :] * u_prev + cw_ref[1:2, :] * u + cw_ref[2:3, :] * u_next
    z = (pb_ref[...].astype(F32) * conv).astype(BF16)
    y_conv = jnp.dot(z, wc_ref[...], preferred_element_type=F32)
    y_attn = jnp.dot(attn_ref[...], wa_ref[...], preferred_element_type=F32)
    merged = ga_ref[...].astype(F32) * y_attn + gc_ref[...].astype(F32) * y_conv
    o_ref[...] = merged.astype(o_ref.dtype)


def _merge(attn, p_all, conv_w, wa_b, wc_b, *, tm, seq, conv_width, d, conv_start_col, gate_start_col):
    rows, attn_width = attn.shape
    hb = BF16_SUBLANES
    cb = conv_start_col // conv_width
    gb = gate_start_col // d
    n_hblocks = rows // hb
    per = tm // hb

    def prev_map(col):
        return lambda i: (jnp.maximum(i * per - 1, 0), col)

    def next_map(col):
        return lambda i: (jnp.minimum((i + 1) * per, n_hblocks - 1), col)

    body = functools.partial(_merge_body, tiles_per_seq=seq // tm)
    resident = dict(pipeline_mode=pl.Buffered(1))
    return pl.pallas_call(
        body,
        grid=(rows // tm,),
        in_specs=[pl.BlockSpec((tm, attn_width), lambda i: (i, 0)),
                  pl.BlockSpec((tm, conv_width), lambda i: (i, cb)),
                  pl.BlockSpec((tm, conv_width), lambda i: (i, cb + 1)),
                  pl.BlockSpec((tm, conv_width), lambda i: (i, cb + 2)),
                  pl.BlockSpec((hb, conv_width), prev_map(cb + 1)),
                  pl.BlockSpec((hb, conv_width), prev_map(cb + 2)),
                  pl.BlockSpec((hb, conv_width), next_map(cb + 1)),
                  pl.BlockSpec((hb, conv_width), next_map(cb + 2)),
                  pl.BlockSpec((tm, d), lambda i: (i, gb)),
                  pl.BlockSpec((tm, d), lambda i: (i, gb + 1)),
                  pl.BlockSpec(conv_w.shape, lambda i: (0, 0)),
                  pl.BlockSpec(wa_b.shape, lambda i: (0, 0), **resident),
                  pl.BlockSpec(wc_b.shape, lambda i: (0, 0), **resident)],
        out_specs=pl.BlockSpec((tm, d), lambda i: (i, 0)),
        out_shape=jax.ShapeDtypeStruct((rows, d), BF16),
        compiler_params=_params(1),
        name="merge",
    )(attn, p_all, p_all, p_all, p_all, p_all, p_all, p_all, p_all, p_all, conv_w, wa_b, wc_b)


def _split_bf16(t):
    hi = t.astype(BF16)
    lo = (t - hi.astype(F32)).astype(BF16)
    return hi, lo


def _mix_route_body(m_ref, wm_ref, x_ref, mod_ref, gpost_ref, gpre_ref, wr_ref, br_ref,
                    h_ref, f_ref, route_ref, cnt_ref, carry_scr, *, n_groups, per_group):
    i = pl.program_id(0)
    tm = m_ref.shape[0]

    @pl.when(i == 0)
    def _():
        carry_scr[...] = jnp.zeros_like(carry_scr)

    y = jnp.dot(m_ref[...], wm_ref[...], preferred_element_type=F32)
    h = x_ref[...] + mod_ref[0, 2:3, :] * _rms(y, gpost_ref[...])
    h_ref[...] = h
    f = _rms(h, gpre_ref[...]) * (1.0 + mod_ref[0, 4:5, :]) + mod_ref[0, 3:4, :]
    f_ref[...] = f

    f_hi, f_lo = _split_bf16(f)
    w_hi, w_lo = _split_bf16(wr_ref[...])
    logits = (jnp.dot(f_hi, w_hi, preferred_element_type=F32)
              + jnp.dot(f_hi, w_lo, preferred_element_type=F32)
              + jnp.dot(f_lo, w_hi, preferred_element_type=F32)) + br_ref[...]

    lane = lax.broadcasted_iota(jnp.int32, logits.shape, 1)

    def first_argmax(vals):
        mx = jnp.max(vals, axis=-1, keepdims=True)
        idx = jnp.min(jnp.where(vals == mx, lane, LANES), axis=-1, keepdims=True)
        return mx, idx

    is_group = lane < n_groups
    g_max, g_idx = first_argmax(jnp.where(is_group, logits, NEG_BIG))
    g_den = jnp.sum(jnp.where(is_group, jnp.exp(logits - g_max), 0.0), axis=-1, keepdims=True)
    grp_p = 1.0 / g_den
    lo_lane = n_groups + g_idx * per_group
    e_logits = jnp.where((lane >= lo_lane) & (lane < lo_lane + per_group), logits, NEG_BIG)
    l1, i1 = first_argmax(e_logits)
    l2, i2 = first_argmax(jnp.where(lane == i1, NEG_BIG, e_logits))
    r = jnp.exp(l2 - l1)
    gate1 = grp_p / (1.0 + r)
    gate2 = grp_p * r / (1.0 + r)
    e1 = i1 - n_groups
    e2 = i2 - n_groups

    onehot = ((lane == e1) | (lane == e2)).astype(BF16)
    r_i = lax.broadcasted_iota(jnp.int32, (tm, tm), 0)
    c_i = lax.broadcasted_iota(jnp.int32, (tm, tm), 1)
    before = (c_i < r_i).astype(BF16)
    seen = jnp.dot(before, onehot, preferred_element_type=F32) + carry_scr[0:1, :]
    rank1 = jnp.sum(jnp.where(lane == e1, seen, 0.0), axis=-1, keepdims=True)
    rank2 = jnp.sum(jnp.where(lane == e2, seen, 0.0), axis=-1, keepdims=True)
    carry_scr[...] = carry_scr[...] + jnp.sum(onehot.astype(F32), axis=0, keepdims=True)
    cnt_ref[...] = carry_scr[...]

    route = jnp.where(lane == 0, e1.astype(F32), 0.0)
    route = jnp.where(lane == 1, e2.astype(F32), route)
    route = jnp.where(lane == 2, gate1, route)
    route = jnp.where(lane == 3, gate2, route)
    route = jnp.where(lane == 4, rank1, route)
    route = jnp.where(lane == 5, rank2, route)
    route_ref[...] = route


def _mix_route(merged, wm_b, x2, mod3, gpost, gpre, w_r, b_r, *, tm, seq, n_groups, per_group):
    rows, d = x2.shape
    per_batch = seq // tm
    body = functools.partial(_mix_route_body, n_groups=n_groups, per_group=per_group)
    return pl.pallas_call(
        body,
        grid=(rows // tm,),
        in_specs=[pl.BlockSpec((tm, d), lambda i: (i, 0)),
                  pl.BlockSpec(wm_b.shape, lambda i: (0, 0), pipeline_mode=pl.Buffered(1)),
                  pl.BlockSpec((tm, d), lambda i: (i, 0)),
                  pl.BlockSpec((1, N_MOD, d), lambda i: (i // per_batch, 0, 0)),
                  pl.BlockSpec((1, d), lambda i: (0, 0)),
                  pl.BlockSpec((1, d), lambda i: (0, 0)),
                  pl.BlockSpec((d, LANES), lambda i: (0, 0)),
                  pl.BlockSpec((1, LANES), lambda i: (0, 0))],
        out_specs=[pl.BlockSpec((tm, d), lambda i: (i, 0)),
                   pl.BlockSpec((tm, d), lambda i: (i, 0)),
                   pl.BlockSpec((tm, LANES), lambda i: (i, 0)),
                   pl.BlockSpec((8, LANES), lambda i: (0, 0))],
        out_shape=[jax.ShapeDtypeStruct((rows, d), F32),
                   jax.ShapeDtypeStruct((rows, d), F32),
                   jax.ShapeDtypeStruct((rows, LANES), F32),
                   jax.ShapeDtypeStruct((8, LANES), F32)],
        scratch_shapes=[pltpu.VMEM((8, LANES), F32)],
        compiler_params=_params(1),
        name="mix_route",
    )(merged, wm_b, x2, mod3, gpost, gpre, w_r, b_r)


DMA_ISSUE_UNROLL = 8


def _dispatch_body(d1_ref, d2_ref, f_hbm, xs_hbm, sem, *, batch_rows):
    n_tok = f_hbm.shape[0]
    n_batches = n_tok // batch_rows

    def issue(b, slot):
        base = b * batch_rows

        def body(r, carry):
            t = base + r
            pltpu.make_async_copy(f_hbm.at[pl.ds(t, 1)], xs_hbm.at[pl.ds(d1_ref[t], 1)], sem.at[slot]).start()
            pltpu.make_async_copy(f_hbm.at[pl.ds(t, 1)], xs_hbm.at[pl.ds(d2_ref[t], 1)], sem.at[slot]).start()
            return carry

        lax.fori_loop(0, batch_rows, body, 0, unroll=DMA_ISSUE_UNROLL)

    def drain(slot):
        rows = TOP_K * batch_rows
        pltpu.make_async_copy(f_hbm.at[pl.ds(0, rows)], xs_hbm.at[pl.ds(0, rows)], sem.at[slot]).wait()

    issue(0, 0)

    def step(b, carry):
        issue(b, b % 2)
        drain((b - 1) % 2)
        return carry

    lax.fori_loop(1, n_batches, step, 0)
    drain((n_batches - 1) % 2)


def _dispatch(dest1, dest2, f, *, n_rows, batch_rows):
    n_tok, d = f.shape
    body = functools.partial(_dispatch_body, batch_rows=batch_rows)
    grid_spec = pltpu.PrefetchScalarGridSpec(
        num_scalar_prefetch=2,
        grid=(1,),
        in_specs=[pl.BlockSpec(memory_space=pl.ANY)],
        out_specs=pl.BlockSpec(memory_space=pl.ANY),
        scratch_shapes=[pltpu.SemaphoreType.DMA((2,))],
    )
    return pl.pallas_call(
        body,
        grid_spec=grid_spec,
        out_shape=jax.ShapeDtypeStruct((n_rows, d), f.dtype),
        compiler_params=_params(1),
        name="dispatch",
    )(dest1, dest2, f)


def _experts_body(be_ref, nu_ref, valid_ref, x_ref, wg_ref, wu_ref, wd_ref, o_ref, wg_b, wu_b, wd_b):
    i = pl.program_id(0)

    @pl.when(i < nu_ref[0])
    def _():
        changed = (i == 0) | (be_ref[i] != be_ref[jnp.maximum(i - 1, 0)])

        @pl.when(changed)
        def _():
            wg_b[...] = wg_ref[0].astype(BF16)
            wu_b[...] = wu_ref[0].astype(BF16)
            wd_b[...] = wd_ref[0].astype(BF16)

        row = lax.broadcasted_iota(jnp.int32, x_ref.shape, 0)
        x = jnp.where(row < valid_ref[i], x_ref[...], 0.0).astype(BF16)
        g = jnp.dot(x, wg_b[...], preferred_element_type=F32)
        u = jnp.dot(x, wu_b[...], preferred_element_type=F32)
        hid = (g * jax.nn.sigmoid(g) * u).astype(BF16)
        o_ref[...] = jnp.dot(hid, wd_b[...], preferred_element_type=F32)


def _experts(block_expert, n_used, block_valid, xs, w_gate, w_up, w_down, *, bm):
    n_blocks = block_expert.shape[0]
    _, d, de = w_gate.shape

    def row_block(i, be, nu, valid):
        return (jnp.minimum(i, nu[0] - 1), 0)

    def weight_block(i, be, nu, valid):
        return (be[i], 0, 0)

    grid_spec = pltpu.PrefetchScalarGridSpec(
        num_scalar_prefetch=3,
        grid=(n_blocks,),
        in_specs=[pl.BlockSpec((bm, d), row_block),
                  pl.BlockSpec((1, d, de), weight_block),
                  pl.BlockSpec((1, d, de), weight_block),
                  pl.BlockSpec((1, de, d), weight_block)],
        out_specs=pl.BlockSpec((bm, d), row_block),
        scratch_shapes=[pltpu.VMEM((d, de), BF16),
                        pltpu.VMEM((d, de), BF16),
                        pltpu.VMEM((de, d), BF16)],
    )
    return pl.pallas_call(
        _experts_body,
        grid_spec=grid_spec,
        out_shape=jax.ShapeDtypeStruct((n_blocks * bm, d), F32),
        compiler_params=_params(1),
        name="experts",
    )(block_expert, n_used, block_valid, xs, w_gate, w_up, w_down)


def _combine_body(d1_ref, d2_ref, yb_hbm, route_ref, h_ref, mod_ref, g_ref, o_ref, ybuf, sem):
    i = pl.program_id(0)
    n = pl.num_programs(0)
    tm = h_ref.shape[0]

    def start_gather(blk, slot):
        base = blk * tm

        def body(r, carry):
            pltpu.make_async_copy(yb_hbm.at[pl.ds(d1_ref[base + r], 1)], ybuf.at[slot, 0, pl.ds(r, 1)],
                                  sem.at[slot]).start()
            pltpu.make_async_copy(yb_hbm.at[pl.ds(d2_ref[base + r], 1)], ybuf.at[slot, 1, pl.ds(r, 1)],
                                  sem.at[slot]).start(priority=1)
            return carry

        lax.fori_loop(0, tm, body, 0, unroll=DMA_ISSUE_UNROLL)

    def wait_gather(slot):
        for k in range(TOP_K):
            pltpu.make_async_copy(yb_hbm.at[pl.ds(0, tm)], ybuf.at[slot, k], sem.at[slot]).wait()

    @pl.when(i == 0)
    def _():
        start_gather(0, 0)

    @pl.when(i + 1 < n)
    def _():
        start_gather(i + 1, (i + 1) % 2)

    slot = i % 2
    wait_gather(slot)
    y = route_ref[:, 2:3] * ybuf[slot, 0] + route_ref[:, 3:4] * ybuf[slot, 1]
    o_ref[...] = h_ref[...] + mod_ref[0, 5:6, :] * _rms(y, g_ref[...])


def _combine(dest1, dest2, yb, route, h, mod3, gpost, *, tm, seq):
    rows, d = h.shape
    per_batch = seq // tm
    grid_spec = pltpu.PrefetchScalarGridSpec(
        num_scalar_prefetch=2,
        grid=(rows // tm,),
        in_specs=[pl.BlockSpec(memory_space=pl.ANY),
                  pl.BlockSpec((tm, LANES), lambda i, d1, d2: (i, 0)),
                  pl.BlockSpec((tm, d), lambda i, d1, d2: (i, 0)),
                  pl.BlockSpec((1, N_MOD, d), lambda i, d1, d2: (i // per_batch, 0, 0)),
                  pl.BlockSpec((1, d), lambda i, d1, d2: (0, 0))],
        out_specs=pl.BlockSpec((tm, d), lambda i, d1, d2: (i, 0)),
        scratch_shapes=[pltpu.VMEM((2, TOP_K, tm, d), F32),
                        pltpu.SemaphoreType.DMA((2,))],
    )
    return pl.pallas_call(
        _combine_body,
        grid_spec=grid_spec,
        out_shape=jax.ShapeDtypeStruct((rows, d), F32),
        compiler_params=_params(1),
        name="combine",
    )(dest1, dest2, yb, route, h, mod3, gpost)


def _rope_tables(seq):
    rows = seq // GRID_W
    row = jnp.repeat(jnp.arange(rows, dtype=F32), GRID_W)
    col = jnp.tile(jnp.arange(GRID_W, dtype=F32), rows)
    pairs = HEAD_DIM // 4
    freqs = ROPE_THETA ** (-jnp.arange(pairs, dtype=F32) / pairs)
    ang = jnp.concatenate([row[:, None] * freqs, col[:, None] * freqs], axis=-1)
    cos, sin = jnp.cos(ang), jnp.sin(ang)
    return jnp.concatenate([cos, cos], axis=-1), jnp.concatenate([-sin, sin], axis=-1)


def _tile(limit, size):
    t = min(limit, size)
    assert size % t == 0, (limit, size)
    return t


def kernel(x, c, ctx, c_ctx, w_ada, b_ada, g_pre_mix, w_in, q_norm, k_norm, conv_w, w_attn_out, w_conv_out,
           w_mix_out, g_post_mix, g_pre_ffn, w_router_group, b_router_group, w_router_expert, b_router_expert,
           w_gate, w_up, w_down, g_post_ffn):
    assert w_ada.shape[0] == 1, "single-layer problem"
    batch, seq, d = x.shape
    n_ctx = ctx.shape[1]
    attn_width = w_attn_out.shape[1]
    conv_width = conv_w.shape[-1]
    in_width = w_in.shape[-1]
    kv_width = (in_width - attn_width - 3 * conv_width - 2 * d) // 2
    n_kv = kv_width // HEAD_DIM
    group = attn_width // kv_width
    conv_start = attn_width + 2 * kv_width
    gate_start = conv_start + 3 * conv_width
    n_groups, per_group = w_router_expert.shape[2], w_router_expert.shape[3]
    n_experts = n_groups * per_group
    assert n_groups + n_experts <= LANES and batch + 1 <= 8
    n_tok = batch * seq

    cc = jnp.concatenate([c, c_ctx[None, :], jnp.zeros((8 - batch - 1, d), F32)], axis=0)
    mod3 = _adaln(cc, w_ada[0], b_ada[0]).reshape(8, N_MOD, d)

    w_in_b = w_in[0].astype(BF16)
    cos2, sin2 = _rope_tables(seq)
    tn = _tile(512, kv_width)
    tm_in = _tile(512, seq)
    x2 = x.reshape(n_tok, d)
    common = dict(tn=tn, attn_width=attn_width, kv_width=kv_width, gate_start_col=gate_start)
    per_batch_in = seq // tm_in
    n_sections = 2
    assert in_width % (n_sections * tn) == 0 and attn_width % (2 * kv_width) == 0
    p_all = _inproj(x2, mod3, g_pre_mix, w_in_b, cos2, sin2, q_norm, k_norm, tm=tm_in,
                    sec_width=in_width // n_sections, sec_off=0, n_sections=n_sections,
                    mod_row=lambda i: i // per_batch_in, pos_blocks=per_batch_in, use_rope=True, **common)
    tm_ctx = _tile(512, n_ctx)
    p_ctx = _inproj(ctx.reshape(batch * n_ctx, d), mod3, g_pre_mix, w_in_b, cos2[:tm_ctx], sin2[:tm_ctx],
                    q_norm, k_norm, tm=tm_ctx, sec_width=2 * kv_width, sec_off=attn_width // (2 * kv_width),
                    n_sections=1, mod_row=lambda i: batch, pos_blocks=1, use_rope=False, **common)

    kv_lat = p_all[:, attn_width:conv_start].reshape(batch, seq, 2 * kv_width)
    kv_all = jnp.concatenate([p_ctx.reshape(batch, n_ctx, 2 * kv_width), kv_lat], axis=1)
    k_all = kv_all[:, :, :kv_width]
    vt_all = kv_all[:, :, kv_width:].reshape(batch, n_ctx + seq, n_kv, HEAD_DIM).transpose(0, 2, 3, 1)
    attn = _attention(p_all, k_all, vt_all, batch=batch, seq=seq, n_kv=n_kv, group=group,
                      tq=_tile(128, seq))

    merged = _merge(attn, p_all, conv_w[0], w_attn_out[0].astype(BF16), w_conv_out[0].astype(BF16),
                    tm=_tile(256, seq), seq=seq, conv_width=conv_width, d=d,
                    conv_start_col=conv_start, gate_start_col=gate_start)

    pad = LANES - n_groups - n_experts
    w_r = jnp.concatenate([w_router_group[0], w_router_expert[0].reshape(d, n_experts), jnp.zeros((d, pad), F32)], axis=1)
    b_r = jnp.concatenate([b_router_group[0], b_router_expert[0].reshape(n_experts), jnp.zeros((pad,), F32)])[None, :]
    tm_tok = _tile(256, seq)
    h, f, route, counts = _mix_route(merged, w_mix_out[0].astype(BF16), x2, mod3, g_post_mix, g_pre_ffn, w_r, b_r,
                                     tm=_tile(512, seq), seq=seq, n_groups=n_groups, per_group=per_group)

    bm = _tile(256, n_tok)
    n_blocks = n_tok * TOP_K // bm + n_experts
    cnt = counts[0, :n_experts].astype(jnp.int32)
    padded = (cnt + bm - 1) // bm * bm
    pend = jnp.cumsum(padded)
    pstart = pend - padded
    e1, e2 = route[:, 0].astype(jnp.int32), route[:, 1].astype(jnp.int32)
    expert_ids = jnp.arange(n_experts, dtype=jnp.int32)[None, :]

    def segment_start(e):
        return jnp.sum(jnp.where(e[:, None] == expert_ids, pstart[None, :], 0), axis=1)

    dest1 = segment_start(e1) + route[:, 4].astype(jnp.int32)
    dest2 = segment_start(e2) + route[:, 5].astype(jnp.int32)
    n_used = (pend[-1] // bm).astype(jnp.int32)
    blk_row = jnp.minimum(jnp.arange(n_blocks, dtype=jnp.int32), n_used - 1) * bm
    blk_exp = jnp.sum((pend[None, :] <= blk_row[:, None]).astype(jnp.int32), axis=1)
    blk_exp = jnp.minimum(blk_exp, n_experts - 1)
    seg_end = jnp.sum(jnp.where(blk_exp[:, None] == expert_ids, (pstart + cnt)[None, :], 0), axis=1)
    blk_valid = jnp.clip(seg_end - blk_row, 0, bm)

    xs = _dispatch(dest1, dest2, f, n_rows=n_blocks * bm, batch_rows=_tile(256, n_tok))
    yb = _experts(blk_exp, n_used.reshape(1), blk_valid, xs, w_gate[0], w_up[0], w_down[0], bm=bm)
    out = _combine(dest1, dest2, yb, route, h, mod3, g_post_ffn, tm=tm_tok, seq=seq)
    return out.reshape(batch, seq, d)
```

```python
import functools

import jax
import jax.numpy as jnp
from jax import lax
from jax.experimental import pallas as pl
from jax.experimental.pallas import tpu as pltpu

HEAD_DIM = 128
GRID_W = 64
ROPE_THETA = 10000.0
EPS = 1e-6
TOP_K = 2
N_MOD = 6
LANES = 128
BF16_SUBLANES = 16
VMEM_LIMIT_BYTES = 56 * 1024 * 1024
NEG_BIG = -1e30
LOG2_E = 1.4426950408889634

F32 = jnp.float32
BF16 = jnp.bfloat16


def _params(n_grid_axes):
    return pltpu.CompilerParams(dimension_semantics=("arbitrary",) * n_grid_axes,
                                vmem_limit_bytes=VMEM_LIMIT_BYTES)


def _rms(t, gain):
    ms = jnp.mean(t * t, axis=-1, keepdims=True)
    return t * lax.rsqrt(ms + EPS) * gain


def _adaln_body(c_ref, w_ref, b_ref, o_ref):
    c = c_ref[...]
    s = (c * jax.nn.sigmoid(c)).astype(BF16)
    o_ref[...] = jnp.dot(s, w_ref[...].astype(BF16), preferred_element_type=F32) + b_ref[...]


def _adaln(cc, w, b):
    rows, d = cc.shape
    n = w.shape[1]
    tn = min(1024, n)
    return pl.pallas_call(
        _adaln_body,
        grid=(n // tn,),
        in_specs=[pl.BlockSpec((rows, d), lambda j: (0, 0)),
                  pl.BlockSpec((d, tn), lambda j: (0, j)),
                  pl.BlockSpec((1, tn), lambda j: (0, j))],
        out_specs=pl.BlockSpec((rows, tn), lambda j: (0, j)),
        out_shape=jax.ShapeDtypeStruct((rows, n), F32),
        compiler_params=_params(1),
        name="adaln",
    )(cc, w, b.reshape(1, n))


Q_COLS, K_COLS, PLAIN_COLS, GATE_COLS = "q", "k", "plain", "gate"


def _inproj_body(x_ref, mod_ref, g_ref, w_ref, cos_ref, sin_ref, qn_ref, kn_ref, o_ref, *,
                 sections, tn, use_rope, q_scale):
    sec = pl.program_id(0)
    xn = _rms(x_ref[...], g_ref[...])
    a = (xn * (1.0 + mod_ref[0, 1:2, :]) + mod_ref[0, 0:1, :]).astype(BF16)

    def normed_heads(acc, col0, gain, out_scale):
        for hh in range(tn // HEAD_DIM):
            t = _rms(acc[:, hh * HEAD_DIM:(hh + 1) * HEAD_DIM], gain)
            if use_rope:
                t = t * cos_ref[...] + pltpu.roll(t, HEAD_DIM // 2, 1) * sin_ref[...]
            if out_scale != 1.0:
                t = t * out_scale
            o_ref[:, col0 + hh * HEAD_DIM:col0 + (hh + 1) * HEAD_DIM] = t.astype(o_ref.dtype)

    def section(kinds):
        for t, kind in enumerate(kinds):
            col0 = t * tn
            acc = jnp.dot(a, w_ref[:, col0:col0 + tn], preferred_element_type=F32)
            if kind == Q_COLS:
                normed_heads(acc, col0, qn_ref[...], q_scale)
            elif kind == K_COLS:
                normed_heads(acc, col0, kn_ref[...], 1.0)
            elif kind == PLAIN_COLS:
                o_ref[:, col0:col0 + tn] = acc.astype(o_ref.dtype)
            else:
                o_ref[:, col0:col0 + tn] = jax.nn.sigmoid(acc).astype(o_ref.dtype)

    for s_idx, kinds in enumerate(sections):
        pl.when(sec == s_idx)(functools.partial(section, kinds))


def _inproj(x2, mod3, g, w_b, cos2, sin2, qn, kn, *, tm, tn, sec_width, sec_off, n_sections, mod_row, pos_blocks,
            attn_width, kv_width, gate_start_col, use_rope):
    rows, d = x2.shape

    def kind_of(col):
        if col < attn_width:
            return Q_COLS
        if col < attn_width + kv_width:
            return K_COLS
        return PLAIN_COLS if col < gate_start_col else GATE_COLS

    sections = tuple(tuple(kind_of((sec_off + s) * sec_width + t * tn) for t in range(sec_width // tn))
                     for s in range(n_sections))
    body = functools.partial(_inproj_body, sections=sections, tn=tn, use_rope=use_rope,
                             q_scale=HEAD_DIM ** -0.5 * LOG2_E)
    return pl.pallas_call(
        body,
        grid=(n_sections, rows // tm),
        in_specs=[pl.BlockSpec((tm, d), lambda s, i: (i, 0)),
                  pl.BlockSpec((1, N_MOD, d), lambda s, i: (mod_row(i), 0, 0)),
                  pl.BlockSpec((1, d), lambda s, i: (0, 0)),
                  pl.BlockSpec((d, sec_width), lambda s, i: (0, s + sec_off), pipeline_mode=pl.Buffered(1)),
                  pl.BlockSpec((tm, HEAD_DIM), lambda s, i: (i % pos_blocks, 0)),
                  pl.BlockSpec((tm, HEAD_DIM), lambda s, i: (i % pos_blocks, 0)),
                  pl.BlockSpec((1, HEAD_DIM), lambda s, i: (0, 0)),
                  pl.BlockSpec((1, HEAD_DIM), lambda s, i: (0, 0))],
        out_specs=pl.BlockSpec((tm, sec_width), lambda s, i: (i, s)),
        out_shape=jax.ShapeDtypeStruct((rows, n_sections * sec_width), BF16),
        compiler_params=_params(2),
        name="inproj_rope" if use_rope else "inproj_ctx",
    )(x2, mod3, g, w_b, cos2, sin2, qn, kn)


def _attn_body(q_ref, k_ref, vt_ref, o_ref, s0_scr, m0_scr, s1_scr, m1_scr, *, group, tq):
    q_tiles = q_ref.shape[0] // tq
    slabs = ((s0_scr, m0_scr), (s1_scr, m1_scr))

    def rows(t):
        start = t * tq
        return pl.ds(start if isinstance(start, int) else pl.multiple_of(start, tq), tq)

    def scores(t, slab):
        s_scr, m_scr = slab
        q = jnp.concatenate([q_ref[rows(t), g * HEAD_DIM:(g + 1) * HEAD_DIM] for g in range(group)], axis=0)
        s = lax.dot_general(k_ref[0], q, (((1,), (1,)), ((), ())), preferred_element_type=F32)
        s_scr[...] = s
        m_scr[...] = jnp.max(s, axis=0, keepdims=True)

    def finish(t, slab):
        s_scr, m_scr = slab
        p = jnp.exp2(s_scr[...] - m_scr[...])
        l = jnp.sum(p, axis=0, keepdims=True)
        acc = jnp.dot(vt_ref[0, 0], p.astype(BF16), preferred_element_type=F32)
        out = acc * (1.0 / l)
        for g in range(group):
            o_ref[rows(t), g * HEAD_DIM:(g + 1) * HEAD_DIM] = out[:, g * tq:(g + 1) * tq].T.astype(o_ref.dtype)

    def step(t, parity):
        scores(t, slabs[parity])
        finish(t - 1, slabs[1 - parity])

    scores(0, slabs[0])
    n_pairs = (q_tiles - 1) // 2

    def pair(j, carry):
        t = 1 + 2 * j
        step(t, 1)
        step(t + 1, 0)
        return carry

    if n_pairs > 0:
        lax.fori_loop(0, n_pairs, pair, 0)
    if (q_tiles - 1) % 2:
        step(q_tiles - 1, (q_tiles - 1) % 2)
    finish(q_tiles - 1, slabs[(q_tiles - 1) % 2])


def _attention(p_all, k_all, vt_all, *, batch, seq, n_kv, group, tq):
    gw = group * HEAD_DIM
    body = functools.partial(_attn_body, group=group, tq=tq)
    n_keys = k_all.shape[1]
    return pl.pallas_call(
        body,
        grid=(batch, n_kv),
        in_specs=[pl.BlockSpec((seq, gw), lambda b, h: (b, h)),
                  pl.BlockSpec((1, n_keys, HEAD_DIM), lambda b, h: (b, 0, h)),
                  pl.BlockSpec((1, 1, HEAD_DIM, n_keys), lambda b, h: (b, h, 0, 0))],
        out_specs=pl.BlockSpec((seq, gw), lambda b, h: (b, h)),
        out_shape=jax.ShapeDtypeStruct((batch * seq, n_kv * gw), BF16),
        scratch_shapes=[pltpu.VMEM((n_keys, group * tq), F32), pltpu.VMEM((1, group * tq), F32)] * 2,
        compiler_params=_params(2),
        name="attention",
    )(p_all, k_all, vt_all)


def _merge_body(attn_ref, pb_ref, pc_ref, px_ref, pcp_ref, pxp_ref, pcn_ref, pxn_ref, ga_ref, gc_ref,
                cw_ref, wa_ref, wc_ref, o_ref, *, tiles_per_seq):
    i = pl.program_id(0)
    tm = attn_ref.shape[0]
    u = pc_ref[...].astype(F32) * px_ref[...].astype(F32)
    last = BF16_SUBLANES - 1
    u_before = pcp_ref[last:last + 1, :].astype(F32) * pxp_ref[last:last + 1, :].astype(F32)
    u_after = pcn_ref[0:1, :].astype(F32) * pxn_ref[0:1, :].astype(F32)
    pos = i % tiles_per_seq
    u_before = jnp.where(pos == 0, 0.0, u_before)
    u_after = jnp.where(pos == tiles_per_seq - 1, 0.0, u_after)
    row = lax.broadcasted_iota(jnp.int32, u.shape, 0)
    u_prev = jnp.where(row == 0, u_before, pltpu.roll(u, 1, 0))
    u_next = jnp.where(row == tm - 1, u_after, pltpu.roll(u, tm - 1, 0))
    conv = cw_ref[0:1, :] * u_prev + cw_ref[1:2, :] * u + cw_ref[2:3, :] * u_next
    z = (pb_ref[...].astype(F32) * conv).astype(BF16)
    y_conv = jnp.dot(z, wc_ref[...], preferred_element_type=F32)
    y_attn = jnp.dot(attn_ref[...], wa_ref[...], preferred_element_type=F32)
    merged = ga_ref[...].astype(F32) * y_attn + gc_ref[...].astype(F32) * y_conv
    o_ref[...] = merged.astype(o_ref.dtype)


def _merge(attn, p_all, conv_w, wa_b, wc_b, *, tm, seq, conv_width, d, conv_start_col, gate_start_col):
    rows, attn_width = attn.shape
    hb = BF16_SUBLANES
    cb = conv_start_col // conv_width
    gb = gate_start_col // d
    n_hblocks = rows // hb
    per = tm // hb

    def prev_map(col):
        return lambda i: (jnp.maximum(i * per - 1, 0), col)

    def next_map(col):
        return lambda i: (jnp.minimum((i + 1) * per, n_hblocks - 1), col)

    body = functools.partial(_merge_body, tiles_per_seq=seq // tm)
    resident = dict(pipeline_mode=pl.Buffered(1))
    return pl.pallas_call(
        body,
        grid=(rows // tm,),
        in_specs=[pl.BlockSpec((tm, attn_width), lambda i: (i, 0)),
                  pl.BlockSpec((tm, conv_width), lambda i: (i, cb)),
                  pl.BlockSpec((tm, conv_width), lambda i: (i, cb + 1)),
                  pl.BlockSpec((tm, conv_width), lambda i: (i, cb + 2)),
                  pl.BlockSpec((hb, conv_width), prev_map(cb + 1)),
                  pl.BlockSpec((hb, conv_width), prev_map(cb + 2)),
                  pl.BlockSpec((hb, conv_width), next_map(cb + 1)),
                  pl.BlockSpec((hb, conv_width), next_map(cb + 2)),
                  pl.BlockSpec((tm, d), lambda i: (i, gb)),
                  pl.BlockSpec((tm, d), lambda i: (i, gb + 1)),
                  pl.BlockSpec(conv_w.shape, lambda i: (0, 0)),
                  pl.BlockSpec(wa_b.shape, lambda i: (0, 0), **resident),
                  pl.BlockSpec(wc_b.shape, lambda i: (0, 0), **resident)],
        out_specs=pl.BlockSpec((tm, d), lambda i: (i, 0)),
        out_shape=jax.ShapeDtypeStruct((rows, d), BF16),
        compiler_params=_params(1),
        name="merge",
    )(attn, p_all, p_all, p_all, p_all, p_all, p_all, p_all, p_all, p_all, conv_w, wa_b, wc_b)


def _split_bf16(t):
    hi = t.astype(BF16)
    lo = (t - hi.astype(F32)).astype(BF16)
    return hi, lo


def _mix_route_body(m_ref, wm_ref, x_ref, mod_ref, gpost_ref, gpre_ref, wr_ref, br_ref,
                    h_ref, f_ref, route_ref, cnt_ref, carry_scr, *, n_groups, per_group):
    i = pl.program_id(0)
    tm = m_ref.shape[0]

    @pl.when(i == 0)
    def _():
        carry_scr[...] = jnp.zeros_like(carry_scr)

    y = jnp.dot(m_ref[...], wm_ref[...], preferred_element_type=F32)
    h = x_ref[...] + mod_ref[0, 2:3, :] * _rms(y, gpost_ref[...])
    h_ref[...] = h
    f = _rms(h, gpre_ref[...]) * (1.0 + mod_ref[0, 4:5, :]) + mod_ref[0, 3:4, :]
    f_ref[...] = f

    f_hi, f_lo = _split_bf16(f)
    w_hi, w_lo = _split_bf16(wr_ref[...])
    logits = (jnp.dot(f_hi, w_hi, preferred_element_type=F32)
              + jnp.dot(f_hi, w_lo, preferred_element_type=F32)
              + jnp.dot(f_lo, w_hi, preferred_element_type=F32)) + br_ref[...]

    lane = lax.broadcasted_iota(jnp.int32, logits.shape, 1)

    def first_argmax(vals):
        mx = jnp.max(vals, axis=-1, keepdims=True)
        idx = jnp.min(jnp.where(vals == mx, lane, LANES), axis=-1, keepdims=True)
        return mx, idx

    is_group = lane < n_groups
    g_max, g_idx = first_argmax(jnp.where(is_group, logits, NEG_BIG))
    g_den = jnp.sum(jnp.where(is_group, jnp.exp(logits - g_max), 0.0), axis=-1, keepdims=True)
    grp_p = 1.0 / g_den
    lo_lane = n_groups + g_idx * per_group
    e_logits = jnp.where((lane >= lo_lane) & (lane < lo_lane + per_group), logits, NEG_BIG)
    l1, i1 = first_argmax(e_logits)
    l2, i2 = first_argmax(jnp.where(lane == i1, NEG_BIG, e_logits))
    r = jnp.exp(l2 - l1)
    gate1 = grp_p / (1.0 + r)
    gate2 = grp_p * r / (1.0 + r)
    e1 = i1 - n_groups
    e2 = i2 - n_groups

    onehot = ((lane == e1) | (lane == e2)).astype(BF16)
    r_i = lax.broadcasted_iota(jnp.int32, (tm, tm), 0)
    c_i = lax.broadcasted_iota(jnp.int32, (tm, tm), 1)
    before = (c_i < r_i).astype(BF16)
    seen = jnp.dot(before, onehot, preferred_element_type=F32) + carry_scr[0:1, :]
    rank1 = jnp.sum(jnp.where(lane == e1, seen, 0.0), axis=-1, keepdims=True)
    rank2 = jnp.sum(jnp.where(lane == e2, seen, 0.0), axis=-1, keepdims=True)
    carry_scr[...] = carry_scr[...] + jnp.sum(onehot.astype(F32), axis=0, keepdims=True)
    cnt_ref[...] = carry_scr[...]

    route = jnp.where(lane == 0, e1.astype(F32), 0.0)
    route = jnp.where(lane == 1, e2.astype(F32), route)
    route = jnp.where(lane == 2, gate1, route)
    route = jnp.where(lane == 3, gate2, route)
    route = jnp.where(lane == 4, rank1, route)
    route = jnp.where(lane == 5, rank2, route)
    route_ref[...] = route


def _mix_route(merged, wm_b, x2, mod3, gpost, gpre, w_r, b_r, *, tm, seq, n_groups, per_group):
    rows, d = x2.shape
    per_batch = seq // tm
    body = functools.partial(_mix_route_body, n_groups=n_groups, per_group=per_group)
    return pl.pallas_call(
        body,
        grid=(rows // tm,),
        in_specs=[pl.BlockSpec((tm, d), lambda i: (i, 0)),
                  pl.BlockSpec(wm_b.shape, lambda i: (0, 0), pipeline_mode=pl.Buffered(1)),
                  pl.BlockSpec((tm, d), lambda i: (i, 0)),
                  pl.BlockSpec((1, N_MOD, d), lambda i: (i // per_batch, 0, 0)),
                  pl.BlockSpec((1, d), lambda i: (0, 0)),
                  pl.BlockSpec((1, d), lambda i: (0, 0)),
                  pl.BlockSpec((d, LANES), lambda i: (0, 0)),
                  pl.BlockSpec((1, LANES), lambda i: (0, 0))],
        out_specs=[pl.BlockSpec((tm, d), lambda i: (i, 0)),
                   pl.BlockSpec((tm, d), lambda i: (i, 0)),
                   pl.BlockSpec((tm, LANES), lambda i: (i, 0)),
                   pl.BlockSpec((8, LANES), lambda i: (0, 0))],
        out_shape=[jax.ShapeDtypeStruct((rows, d), F32),
                   jax.ShapeDtypeStruct((rows, d), F32),
                   jax.ShapeDtypeStruct((rows, LANES), F32),
                   jax.ShapeDtypeStruct((8, LANES), F32)],
        scratch_shapes=[pltpu.VMEM((8, LANES), F32)],
        compiler_params=_params(1),
        name="mix_route",
    )(merged, wm_b, x2, mod3, gpost, gpre, w_r, b_r)


DMA_ISSUE_UNROLL = 8


def _dispatch_body(d1_ref, d2_ref, f_ref, xs_hbm, stage, sem, *, n_steps):
    i = pl.program_id(0)
    tm = f_ref.shape[0]
    slot = i % 2

    def drain(s):
        for _ in range(TOP_K):
            pltpu.make_async_copy(stage.at[s], xs_hbm.at[pl.ds(0, tm)], sem.at[s]).wait()

    @pl.when(i >= 2)
    def _():
        drain(slot)

    stage[slot] = f_ref[...]
    base = i * tm

    def body(r, carry):
        t = base + r
        src = stage.at[slot, pl.ds(r, 1)]
        pltpu.make_async_copy(src, xs_hbm.at[pl.ds(d1_ref[t], 1)], sem.at[slot]).start()
        pltpu.make_async_copy(src, xs_hbm.at[pl.ds(d2_ref[t], 1)], sem.at[slot]).start(priority=1)
        return carry

    lax.fori_loop(0, tm, body, 0, unroll=DMA_ISSUE_UNROLL)

    @pl.when(i == n_steps - 1)
    def _():
        drain(slot)
        if n_steps >= 2:
            drain(1 - slot)


def _dispatch(dest1, dest2, f, *, n_rows, tm):
    n_tok, d = f.shape
    n_steps = n_tok // tm
    body = functools.partial(_dispatch_body, n_steps=n_steps)
    grid_spec = pltpu.PrefetchScalarGridSpec(
        num_scalar_prefetch=2,
        grid=(n_steps,),
        in_specs=[pl.BlockSpec((tm, d), lambda i, d1, d2: (i, 0))],
        out_specs=pl.BlockSpec(memory_space=pl.ANY),
        scratch_shapes=[pltpu.VMEM((2, tm, d), f.dtype),
                        pltpu.SemaphoreType.DMA((2,))],
    )
    return pl.pallas_call(
        body,
        grid_spec=grid_spec,
        out_shape=jax.ShapeDtypeStruct((n_rows, d), f.dtype),
        compiler_params=_params(1),
        name="dispatch",
    )(dest1, dest2, f)


def _experts_body(be_ref, nu_ref, valid_ref, x_ref, wg_ref, wu_ref, wd_ref, o_ref, wg_b, wu_b, wd_b):
    i = pl.program_id(0)

    @pl.when(i < nu_ref[0])
    def _():
        changed = (i == 0) | (be_ref[i] != be_ref[jnp.maximum(i - 1, 0)])

        @pl.when(changed)
        def _():
            wg_b[...] = wg_ref[0].astype(BF16)
            wu_b[...] = wu_ref[0].astype(BF16)
            wd_b[...] = wd_ref[0].astype(BF16)

        row = lax.broadcasted_iota(jnp.int32, x_ref.shape, 0)
        x = jnp.where(row < valid_ref[i], x_ref[...], 0.0).astype(BF16)
        g = jnp.dot(x, wg_b[...], preferred_element_type=F32)
        u = jnp.dot(x, wu_b[...], preferred_element_type=F32)
        hid = (g * jax.nn.sigmoid(g) * u).astype(BF16)
        o_ref[...] = jnp.dot(hid, wd_b[...], preferred_element_type=F32)


def _experts(block_expert, n_used, block_valid, xs, w_gate, w_up, w_down, *, bm):
    n_blocks = block_expert.shape[0]
    _, d, de = w_gate.shape

    def row_block(i, be, nu, valid):
        return (jnp.minimum(i, nu[0] - 1), 0)

    def weight_block(i, be, nu, valid):
        return (be[i], 0, 0)

    grid_spec = pltpu.PrefetchScalarGridSpec(
        num_scalar_prefetch=3,
        grid=(n_blocks,),
        in_specs=[pl.BlockSpec((bm, d), row_block),
                  pl.BlockSpec((1, d, de), weight_block),
                  pl.BlockSpec((1, d, de), weight_block),
                  pl.BlockSpec((1, de, d), weight_block)],
        out_specs=pl.BlockSpec((bm, d), row_block),
        scratch_shapes=[pltpu.VMEM((d, de), BF16),
                        pltpu.VMEM((d, de), BF16),
                        pltpu.VMEM((de, d), BF16)],
    )
    return pl.pallas_call(
        _experts_body,
        grid_spec=grid_spec,
        out_shape=jax.ShapeDtypeStruct((n_blocks * bm, d), F32),
        compiler_params=_params(1),
        name="experts",
    )(block_expert, n_used, block_valid, xs, w_gate, w_up, w_down)


def _combine_body(d1_ref, d2_ref, yb_hbm, route_ref, h_ref, mod_ref, g_ref, o_ref, ybuf, sem):
    i = pl.program_id(0)
    n = pl.num_programs(0)
    tm = h_ref.shape[0]

    def start_gather(blk, slot):
        base = blk * tm

        def body(r, carry):
            pltpu.make_async_copy(yb_hbm.at[pl.ds(d1_ref[base + r], 1)], ybuf.at[slot, 0, pl.ds(r, 1)],
                                  sem.at[slot]).start()
            pltpu.make_async_copy(yb_hbm.at[pl.ds(d2_ref[base + r], 1)], ybuf.at[slot, 1, pl.ds(r, 1)],
                                  sem.at[slot]).start(priority=1)
            return carry

        lax.fori_loop(0, tm, body, 0, unroll=DMA_ISSUE_UNROLL)

    def wait_gather(slot):
        for k in range(TOP_K):
            pltpu.make_async_copy(yb_hbm.at[pl.ds(0, tm)], ybuf.at[slot, k], sem.at[slot]).wait()

    @pl.when(i == 0)
    def _():
        start_gather(0, 0)

    @pl.when(i + 1 < n)
    def _():
        start_gather(i + 1, (i + 1) % 2)

    slot = i % 2
    wait_gather(slot)
    y = route_ref[:, 2:3] * ybuf[slot, 0] + route_ref[:, 3:4] * ybuf[slot, 1]
    o_ref[...] = h_ref[...] + mod_ref[0, 5:6, :] * _rms(y, g_ref[...])


def _combine(dest1, dest2, yb, route, h, mod3, gpost, *, tm, seq):
    rows, d = h.shape
    per_batch = seq // tm
    grid_spec = pltpu.PrefetchScalarGridSpec(
        num_scalar_prefetch=2,
        grid=(rows // tm,),
        in_specs=[pl.BlockSpec(memory_space=pl.ANY),
                  pl.BlockSpec((tm, LANES), lambda i, d1, d2: (i, 0)),
                  pl.BlockSpec((tm, d), lambda i, d1, d2: (i, 0)),
                  pl.BlockSpec((1, N_MOD, d), lambda i, d1, d2: (i // per_batch, 0, 0)),
                  pl.BlockSpec((1, d), lambda i, d1, d2: (0, 0))],
        out_specs=pl.BlockSpec((tm, d), lambda i, d1, d2: (i, 0)),
        scratch_shapes=[pltpu.VMEM((2, TOP_K, tm, d), F32),
                        pltpu.SemaphoreType.DMA((2,))],
    )
    return pl.pallas_call(
        _combine_body,
        grid_spec=grid_spec,
        out_shape=jax.ShapeDtypeStruct((rows, d), F32),
        compiler_params=_params(1),
        name="combine",
    )(dest1, dest2, yb, route, h, mod3, gpost)


def _rope_tables(seq):
    rows = seq // GRID_W
    row = jnp.repeat(jnp.arange(rows, dtype=F32), GRID_W)
    col = jnp.tile(jnp.arange(GRID_W, dtype=F32), rows)
    pairs = HEAD_DIM // 4
    freqs = ROPE_THETA ** (-jnp.arange(pairs, dtype=F32) / pairs)
    ang = jnp.concatenate([row[:, None] * freqs, col[:, None] * freqs], axis=-1)
    cos, sin = jnp.cos(ang), jnp.sin(ang)
    return jnp.concatenate([cos, cos], axis=-1), jnp.concatenate([-sin, sin], axis=-1)


def _tile(limit, size):
    t = min(limit, size)
    assert size % t == 0, (limit, size)
    return t


def kernel(x, c, ctx, c_ctx, w_ada, b_ada, g_pre_mix, w_in, q_norm, k_norm, conv_w, w_attn_out, w_conv_out,
           w_mix_out, g_post_mix, g_pre_ffn, w_router_group, b_router_group, w_router_expert, b_router_expert,
           w_gate, w_up, w_down, g_post_ffn):
    assert w_ada.shape[0] == 1, "single-layer problem"
    batch, seq, d = x.shape
    n_ctx = ctx.shape[1]
    attn_width = w_attn_out.shape[1]
    conv_width = conv_w.shape[-1]
    in_width = w_in.shape[-1]
    kv_width = (in_width - attn_width - 3 * conv_width - 2 * d) // 2
    n_kv = kv_width // HEAD_DIM
    group = attn_width // kv_width
    conv_start = attn_width + 2 * kv_width
    gate_start = conv_start + 3 * conv_width
    n_groups, per_group = w_router_expert.shape[2], w_router_expert.shape[3]
    n_experts = n_groups * per_group
    assert n_groups + n_experts <= LANES and batch + 1 <= 8
    n_tok = batch * seq

    cc = jnp.concatenate([c, c_ctx[None, :], jnp.zeros((8 - batch - 1, d), F32)], axis=0)
    mod3 = _adaln(cc, w_ada[0], b_ada[0]).reshape(8, N_MOD, d)

    w_in_b = w_in[0].astype(BF16)
    cos2, sin2 = _rope_tables(seq)
    tn = _tile(512, kv_width)
    tm_in = _tile(512, seq)
    x2 = x.reshape(n_tok, d)
    common = dict(tn=tn, attn_width=attn_width, kv_width=kv_width, gate_start_col=gate_start)
    per_batch_in = seq // tm_in
    n_sections = 2
    assert in_width % (n_sections * tn) == 0 and attn_width % (2 * kv_width) == 0
    p_all = _inproj(x2, mod3, g_pre_mix, w_in_b, cos2, sin2, q_norm, k_norm, tm=tm_in,
                    sec_width=in_width // n_sections, sec_off=0, n_sections=n_sections,
                    mod_row=lambda i: i // per_batch_in, pos_blocks=per_batch_in, use_rope=True, **common)
    tm_ctx = _tile(512, n_ctx)
    p_ctx = _inproj(ctx.reshape(batch * n_ctx, d), mod3, g_pre_mix, w_in_b, cos2[:tm_ctx], sin2[:tm_ctx],
                    q_norm, k_norm, tm=tm_ctx, sec_width=2 * kv_width, sec_off=attn_width // (2 * kv_width),
                    n_sections=1, mod_row=lambda i: batch, pos_blocks=1, use_rope=False, **common)

    kv_lat = p_all[:, attn_width:conv_start].reshape(batch, seq, 2 * kv_width)
    kv_all = jnp.concatenate([p_ctx.reshape(batch, n_ctx, 2 * kv_width), kv_lat], axis=1)
    k_all = kv_all[:, :, :kv_width]
    vt_all = kv_all[:, :, kv_width:].reshape(batch, n_ctx + seq, n_kv, HEAD_DIM).transpose(0, 2, 3, 1)
    attn = _attention(p_all, k_all, vt_all, batch=batch, seq=seq, n_kv=n_kv, group=group,
                      tq=_tile(128, seq))

    merged = _merge(attn, p_all, conv_w[0], w_attn_out[0].astype(BF16), w_conv_out[0].astype(BF16),
                    tm=_tile(256, seq), seq=seq, conv_width=conv_width, d=d,
                    conv_start_col=conv_start, gate_start_col=gate_start)

    pad = LANES - n_groups - n_experts
    w_r = jnp.concatenate([w_router_group[0], w_router_expert[0].reshape(d, n_experts), jnp.zeros((d, pad), F32)], axis=1)
    b_r = jnp.concatenate([b_router_group[0], b_router_expert[0].reshape(n_experts), jnp.zeros((pad,), F32)])[None, :]
    tm_tok = _tile(256, seq)
    h, f, route, counts = _mix_route(merged, w_mix_out[0].astype(BF16), x2, mod3, g_post_mix, g_pre_ffn, w_r, b_r,
                                     tm=_tile(512, seq), seq=seq, n_groups=n_groups, per_group=per_group)

    bm = _tile(256, n_tok)
    n_blocks = n_tok * TOP_K // bm + n_experts
    cnt = counts[0, :n_experts].astype(jnp.int32)
    padded = (cnt + bm - 1) // bm * bm
    pend = jnp.cumsum(padded)
    pstart = pend - padded
    e1, e2 = route[:, 0].astype(jnp.int32), route[:, 1].astype(jnp.int32)
    expert_ids = jnp.arange(n_experts, dtype=jnp.int32)[None, :]

    def segment_start(e):
        return jnp.sum(jnp.where(e[:, None] == expert_ids, pstart[None, :], 0), axis=1)

    dest1 = segment_start(e1) + route[:, 4].astype(jnp.int32)
    dest2 = segment_start(e2) + route[:, 5].astype(jnp.int32)
    n_used = (pend[-1] // bm).astype(jnp.int32)
    blk_row = jnp.minimum(jnp.arange(n_blocks, dtype=jnp.int32), n_used - 1) * bm
    blk_exp = jnp.sum((pend[None, :] <= blk_row[:, None]).astype(jnp.int32), axis=1)
    blk_exp = jnp.minimum(blk_exp, n_experts - 1)
    seg_end = jnp.sum(jnp.where(blk_exp[:, None] == expert_ids, (pstart + cnt)[None, :], 0), axis=1)
    blk_valid = jnp.clip(seg_end - blk_row, 0, bm)

    xs = _dispatch(dest1, dest2, f, n_rows=n_blocks * bm, tm=tm_tok)
    yb = _experts(blk_exp, n_used.reshape(1), blk_valid, xs, w_gate[0], w_up[0], w_down[0], bm=bm)
    out = _combine(dest1, dest2, yb, route, h, mod3, g_post_ffn, tm=tm_tok, seq=seq)
    return out.reshape(batch, seq, d)
```

```python
import functools

import jax
import jax.numpy as jnp
from jax import lax
from jax.experimental import pallas as pl
from jax.experimental.pallas import tpu as pltpu

HEAD_DIM = 128
GRID_W = 64
ROPE_THETA = 10000.0
EPS = 1e-6
TOP_K = 2
N_MOD = 6
LANES = 128
BF16_SUBLANES = 16
VMEM_LIMIT_BYTES = 56 * 1024 * 1024
NEG_BIG = -1e30
LOG2_E = 1.4426950408889634

F32 = jnp.float32
BF16 = jnp.bfloat16


def _params(n_grid_axes):
    return pltpu.CompilerParams(dimension_semantics=("arbitrary",) * n_grid_axes,
                                vmem_limit_bytes=VMEM_LIMIT_BYTES)


def _rms(t, gain):
    ms = jnp.mean(t * t, axis=-1, keepdims=True)
    return t * lax.rsqrt(ms + EPS) * gain


def _adaln_body(c_ref, w_ref, b_ref, o_ref):
    c = c_ref[...]
    s = (c * jax.nn.sigmoid(c)).astype(BF16)
    o_ref[...] = jnp.dot(s, w_ref[...].astype(BF16), preferred_element_type=F32) + b_ref[...]


def _adaln(cc, w, b):
    rows, d = cc.shape
    n = w.shape[1]
    tn = min(1024, n)
    return pl.pallas_call(
        _adaln_body,
        grid=(n // tn,),
        in_specs=[pl.BlockSpec((rows, d), lambda j: (0, 0)),
                  pl.BlockSpec((d, tn), lambda j: (0, j)),
                  pl.BlockSpec((1, tn), lambda j: (0, j))],
        out_specs=pl.BlockSpec((rows, tn), lambda j: (0, j)),
        out_shape=jax.ShapeDtypeStruct((rows, n), F32),
        compiler_params=_params(1),
        name="adaln",
    )(cc, w, b.reshape(1, n))


Q_COLS, K_COLS, PLAIN_COLS, GATE_COLS = "q", "k", "plain", "gate"


def _inproj_body(x_ref, mod_ref, g_ref, w_ref, cos_ref, sin_ref, qn_ref, kn_ref, o_ref, *,
                 sections, tn, use_rope, q_scale):
    sec = pl.program_id(0)
    xn = _rms(x_ref[...], g_ref[...])
    a = (xn * (1.0 + mod_ref[0, 1:2, :]) + mod_ref[0, 0:1, :]).astype(BF16)

    def normed_heads(acc, col0, gain, out_scale):
        for hh in range(tn // HEAD_DIM):
            t = _rms(acc[:, hh * HEAD_DIM:(hh + 1) * HEAD_DIM], gain)
            if use_rope:
                t = t * cos_ref[...] + pltpu.roll(t, HEAD_DIM // 2, 1) * sin_ref[...]
            if out_scale != 1.0:
                t = t * out_scale
            o_ref[:, col0 + hh * HEAD_DIM:col0 + (hh + 1) * HEAD_DIM] = t.astype(o_ref.dtype)

    def section(kinds):
        for t, kind in enumerate(kinds):
            col0 = t * tn
            acc = jnp.dot(a, w_ref[:, col0:col0 + tn], preferred_element_type=F32)
            if kind == Q_COLS:
                normed_heads(acc, col0, qn_ref[...], q_scale)
            elif kind == K_COLS:
                normed_heads(acc, col0, kn_ref[...], 1.0)
            elif kind == PLAIN_COLS:
                o_ref[:, col0:col0 + tn] = acc.astype(o_ref.dtype)
            else:
                o_ref[:, col0:col0 + tn] = jax.nn.sigmoid(acc).astype(o_ref.dtype)

    for s_idx, kinds in enumerate(sections):
        pl.when(sec == s_idx)(functools.partial(section, kinds))


def _inproj(x2, mod3, g, w_b, cos2, sin2, qn, kn, *, tm, tn, sec_width, sec_off, n_sections, mod_row, pos_blocks,
            attn_width, kv_width, gate_start_col, use_rope):
    rows, d = x2.shape

    def kind_of(col):
        if col < attn_width:
            return Q_COLS
        if col < attn_width + kv_width:
            return K_COLS
        return PLAIN_COLS if col < gate_start_col else GATE_COLS

    sections = tuple(tuple(kind_of((sec_off + s) * sec_width + t * tn) for t in range(sec_width // tn))
                     for s in range(n_sections))
    body = functools.partial(_inproj_body, sections=sections, tn=tn, use_rope=use_rope,
                             q_scale=HEAD_DIM ** -0.5 * LOG2_E)
    return pl.pallas_call(
        body,
        grid=(n_sections, rows // tm),
        in_specs=[pl.BlockSpec((tm, d), lambda s, i: (i, 0)),
                  pl.BlockSpec((1, N_MOD, d), lambda s, i: (mod_row(i), 0, 0)),
                  pl.BlockSpec((1, d), lambda s, i: (0, 0)),
                  pl.BlockSpec((d, sec_width), lambda s, i: (0, s + sec_off), pipeline_mode=pl.Buffered(1)),
                  pl.BlockSpec((tm, HEAD_DIM), lambda s, i: (i % pos_blocks, 0)),
                  pl.BlockSpec((tm, HEAD_DIM), lambda s, i: (i % pos_blocks, 0)),
                  pl.BlockSpec((1, HEAD_DIM), lambda s, i: (0, 0)),
                  pl.BlockSpec((1, HEAD_DIM), lambda s, i: (0, 0))],
        out_specs=pl.BlockSpec((tm, sec_width), lambda s, i: (i, s)),
        out_shape=jax.ShapeDtypeStruct((rows, n_sections * sec_width), BF16),
        compiler_params=_params(2),
        name="inproj_rope" if use_rope else "inproj_ctx",
    )(x2, mod3, g, w_b, cos2, sin2, qn, kn)


def _attn_body(q_ref, k_ref, v_ref, kc_ref, vc_ref, o_ref, s0_scr, m0_scr, s1_scr, m1_scr, vt_scr, *, group, tq):
    seq = q_ref.shape[0]
    q_tiles = seq // tq
    slabs = ((s0_scr, m0_scr), (s1_scr, m1_scr))
    vt_scr[:, :seq] = v_ref[...].astype(F32).T.astype(BF16)
    vt_scr[:, seq:] = vc_ref[...].astype(F32).T.astype(BF16)

    def rows(t):
        start = t * tq
        return pl.ds(start if isinstance(start, int) else pl.multiple_of(start, tq), tq)

    def scores(t, slab):
        s_scr, m_scr = slab
        q = jnp.concatenate([q_ref[rows(t), g * HEAD_DIM:(g + 1) * HEAD_DIM] for g in range(group)], axis=0)
        nt = (((1,), (1,)), ((), ()))
        s_lat = lax.dot_general(k_ref[...], q, nt, preferred_element_type=F32)
        s_ctx = lax.dot_general(kc_ref[...], q, nt, preferred_element_type=F32)
        s_scr[:seq] = s_lat
        s_scr[seq:] = s_ctx
        m_scr[...] = jnp.maximum(jnp.max(s_lat, axis=0, keepdims=True), jnp.max(s_ctx, axis=0, keepdims=True))

    def finish(t, slab):
        s_scr, m_scr = slab
        p = jnp.exp2(s_scr[...] - m_scr[...])
        l = jnp.sum(p, axis=0, keepdims=True)
        acc = jnp.dot(vt_scr[...], p.astype(BF16), preferred_element_type=F32)
        out = acc * (1.0 / l)
        for g in range(group):
            o_ref[rows(t), g * HEAD_DIM:(g + 1) * HEAD_DIM] = out[:, g * tq:(g + 1) * tq].T.astype(o_ref.dtype)

    def step(t, parity):
        scores(t, slabs[parity])
        finish(t - 1, slabs[1 - parity])

    scores(0, slabs[0])
    n_pairs = (q_tiles - 1) // 2

    def pair(j, carry):
        t = 1 + 2 * j
        step(t, 1)
        step(t + 1, 0)
        return carry

    if n_pairs > 0:
        lax.fori_loop(0, n_pairs, pair, 0)
    if (q_tiles - 1) % 2:
        step(q_tiles - 1, (q_tiles - 1) % 2)
    finish(q_tiles - 1, slabs[(q_tiles - 1) % 2])


def _attention(p_all, p_ctx, *, batch, seq, n_ctx, n_kv, group, tq):
    gw = group * HEAD_DIM
    body = functools.partial(_attn_body, group=group, tq=tq)
    n_keys = seq + n_ctx
    k_col = n_kv * group
    v_col = k_col + n_kv
    return pl.pallas_call(
        body,
        grid=(batch, n_kv),
        in_specs=[pl.BlockSpec((seq, gw), lambda b, h: (b, h)),
                  pl.BlockSpec((seq, HEAD_DIM), lambda b, h: (b, k_col + h)),
                  pl.BlockSpec((seq, HEAD_DIM), lambda b, h: (b, v_col + h)),
                  pl.BlockSpec((n_ctx, HEAD_DIM), lambda b, h: (b, h)),
                  pl.BlockSpec((n_ctx, HEAD_DIM), lambda b, h: (b, n_kv + h))],
        out_specs=pl.BlockSpec((seq, gw), lambda b, h: (b, h)),
        out_shape=jax.ShapeDtypeStruct((batch * seq, n_kv * gw), BF16),
        scratch_shapes=[pltpu.VMEM((n_keys, group * tq), F32), pltpu.VMEM((1, group * tq), F32)] * 2
        + [pltpu.VMEM((HEAD_DIM, n_keys), BF16)],
        compiler_params=_params(2),
        name="attention",
    )(p_all, p_all, p_all, p_ctx, p_ctx)


def _merge_body(attn_ref, pb_ref, pc_ref, px_ref, pcp_ref, pxp_ref, pcn_ref, pxn_ref, ga_ref, gc_ref,
                cw_ref, wa_ref, wc_ref, o_ref, *, tiles_per_seq):
    i = pl.program_id(0)
    tm = attn_ref.shape[0]
    u = pc_ref[...].astype(F32) * px_ref[...].astype(F32)
    last = BF16_SUBLANES - 1
    u_before = pcp_ref[last:last + 1, :].astype(F32) * pxp_ref[last:last + 1, :].astype(F32)
    u_after = pcn_ref[0:1, :].astype(F32) * pxn_ref[0:1, :].astype(F32)
    pos = i % tiles_per_seq
    u_before = jnp.where(pos == 0, 0.0, u_before)
    u_after = jnp.where(pos == tiles_per_seq - 1, 0.0, u_after)
    row = lax.broadcasted_iota(jnp.int32, u.shape, 0)
    u_prev = jnp.where(row == 0, u_before, pltpu.roll(u, 1, 0))
    u_next = jnp.where(row == tm - 1, u_after, pltpu.roll(u, tm - 1, 0))
    conv = cw_ref[0:1, :] * u_prev + cw_ref[1:2, :] * u + cw_ref[2:3, :] * u_next
    z = (pb_ref[...].astype(F32) * conv).astype(BF16)
    y_conv = jnp.dot(z, wc_ref[...], preferred_element_type=F32)
    y_attn = jnp.dot(attn_ref[...], wa_ref[...], preferred_element_type=F32)
    merged = ga_ref[...].astype(F32) * y_attn + gc_ref[...].astype(F32) * y_conv
    o_ref[...] = merged.astype(o_ref.dtype)


def _merge(attn, p_all, conv_w, wa_b, wc_b, *, tm, seq, conv_width, d, conv_start_col, gate_start_col):
    rows, attn_width = attn.shape
    hb = BF16_SUBLANES
    cb = conv_start_col // conv_width
    gb = gate_start_col // d
    n_hblocks = rows // hb
    per = tm // hb

    def prev_map(col):
        return lambda i: (jnp.maximum(i * per - 1, 0), col)

    def next_map(col):
        return lambda i: (jnp.minimum((i + 1) * per, n_hblocks - 1), col)

    body = functools.partial(_merge_body, tiles_per_seq=seq // tm)
    resident = dict(pipeline_mode=pl.Buffered(1))
    return pl.pallas_call(
        body,
        grid=(rows // tm,),
        in_specs=[pl.BlockSpec((tm, attn_width), lambda i: (i, 0)),
                  pl.BlockSpec((tm, conv_width), lambda i: (i, cb)),
                  pl.BlockSpec((tm, conv_width), lambda i: (i, cb + 1)),
                  pl.BlockSpec((tm, conv_width), lambda i: (i, cb + 2)),
                  pl.BlockSpec((hb, conv_width), prev_map(cb + 1)),
                  pl.BlockSpec((hb, conv_width), prev_map(cb + 2)),
                  pl.BlockSpec((hb, conv_width), next_map(cb + 1)),
                  pl.BlockSpec((hb, conv_width), next_map(cb + 2)),
                  pl.BlockSpec((tm, d), lambda i: (i, gb)),
                  pl.BlockSpec((tm, d), lambda i: (i, gb + 1)),
                  pl.BlockSpec(conv_w.shape, lambda i: (0, 0)),
                  pl.BlockSpec(wa_b.shape, lambda i: (0, 0), **resident),
                  pl.BlockSpec(wc_b.shape, lambda i: (0, 0), **resident)],
        out_specs=pl.BlockSpec((tm, d), lambda i: (i, 0)),
        out_shape=jax.ShapeDtypeStruct((rows, d), BF16),
        compiler_params=_params(1),
        name="merge",
    )(attn, p_all, p_all, p_all, p_all, p_all, p_all, p_all, p_all, p_all, conv_w, wa_b, wc_b)


def _split_bf16(t):
    hi = t.astype(BF16)
    lo = (t - hi.astype(F32)).astype(BF16)
    return hi, lo


def _mix_route_body(m_ref, wm_ref, x_ref, mod_ref, gpost_ref, gpre_ref, wr_ref, br_ref,
                    h_ref, f_ref, route_ref, cnt_ref, carry_scr, *, n_groups, per_group):
    i = pl.program_id(0)
    tm = m_ref.shape[0]

    @pl.when(i == 0)
    def _():
        carry_scr[...] = jnp.zeros_like(carry_scr)

    y = jnp.dot(m_ref[...], wm_ref[...], preferred_element_type=F32)
    h = x_ref[...] + mod_ref[0, 2:3, :] * _rms(y, gpost_ref[...])
    h_ref[...] = h
    f = _rms(h, gpre_ref[...]) * (1.0 + mod_ref[0, 4:5, :]) + mod_ref[0, 3:4, :]
    f_ref[...] = f

    f_hi, f_lo = _split_bf16(f)
    w_hi, w_lo = _split_bf16(wr_ref[...])
    logits = (jnp.dot(f_hi, w_hi, preferred_element_type=F32)
              + jnp.dot(f_hi, w_lo, preferred_element_type=F32)
              + jnp.dot(f_lo, w_hi, preferred_element_type=F32)) + br_ref[...]

    lane = lax.broadcasted_iota(jnp.int32, logits.shape, 1)

    def first_argmax(vals):
        mx = jnp.max(vals, axis=-1, keepdims=True)
        idx = jnp.min(jnp.where(vals == mx, lane, LANES), axis=-1, keepdims=True)
        return mx, idx

    is_group = lane < n_groups
    g_max, g_idx = first_argmax(jnp.where(is_group, logits, NEG_BIG))
    g_den = jnp.sum(jnp.where(is_group, jnp.exp(logits - g_max), 0.0), axis=-1, keepdims=True)
    grp_p = 1.0 / g_den
    lo_lane = n_groups + g_idx * per_group
    e_logits = jnp.where((lane >= lo_lane) & (lane < lo_lane + per_group), logits, NEG_BIG)
    l1, i1 = first_argmax(e_logits)
    l2, i2 = first_argmax(jnp.where(lane == i1, NEG_BIG, e_logits))
    r = jnp.exp(l2 - l1)
    gate1 = grp_p / (1.0 + r)
    gate2 = grp_p * r / (1.0 + r)
    e1 = i1 - n_groups
    e2 = i2 - n_groups

    onehot = ((lane == e1) | (lane == e2)).astype(BF16)
    r_i = lax.broadcasted_iota(jnp.int32, (tm, tm), 0)
    c_i = lax.broadcasted_iota(jnp.int32, (tm, tm), 1)
    before = (c_i < r_i).astype(BF16)
    seen = jnp.dot(before, onehot, preferred_element_type=F32) + carry_scr[0:1, :]
    rank1 = jnp.sum(jnp.where(lane == e1, seen, 0.0), axis=-1, keepdims=True)
    rank2 = jnp.sum(jnp.where(lane == e2, seen, 0.0), axis=-1, keepdims=True)
    carry_scr[...] = carry_scr[...] + jnp.sum(onehot.astype(F32), axis=0, keepdims=True)
    cnt_ref[...] = carry_scr[...]

    route = jnp.where(lane == 0, e1.astype(F32), 0.0)
    route = jnp.where(lane == 1, e2.astype(F32), route)
    route = jnp.where(lane == 2, gate1, route)
    route = jnp.where(lane == 3, gate2, route)
    route = jnp.where(lane == 4, rank1, route)
    route = jnp.where(lane == 5, rank2, route)
    route_ref[...] = route


def _mix_route(merged, wm_b, x2, mod3, gpost, gpre, w_r, b_r, *, tm, seq, n_groups, per_group):
    rows, d = x2.shape
    per_batch = seq // tm
    body = functools.partial(_mix_route_body, n_groups=n_groups, per_group=per_group)
    return pl.pallas_call(
        body,
        grid=(rows // tm,),
        in_specs=[pl.BlockSpec((tm, d), lambda i: (i, 0)),
                  pl.BlockSpec(wm_b.shape, lambda i: (0, 0), pipeline_mode=pl.Buffered(1)),
                  pl.BlockSpec((tm, d), lambda i: (i, 0)),
                  pl.BlockSpec((1, N_MOD, d), lambda i: (i // per_batch, 0, 0)),
                  pl.BlockSpec((1, d), lambda i: (0, 0)),
                  pl.BlockSpec((1, d), lambda i: (0, 0)),
                  pl.BlockSpec((d, LANES), lambda i: (0, 0)),
                  pl.BlockSpec((1, LANES), lambda i: (0, 0))],
        out_specs=[pl.BlockSpec((tm, d), lambda i: (i, 0)),
                   pl.BlockSpec((tm, d), lambda i: (i, 0)),
                   pl.BlockSpec((tm, LANES), lambda i: (i, 0)),
                   pl.BlockSpec((8, LANES), lambda i: (0, 0))],
        out_shape=[jax.ShapeDtypeStruct((rows, d), F32),
                   jax.ShapeDtypeStruct((rows, d), F32),
                   jax.ShapeDtypeStruct((rows, LANES), F32),
                   jax.ShapeDtypeStruct((8, LANES), F32)],
        scratch_shapes=[pltpu.VMEM((8, LANES), F32)],
        compiler_params=_params(1),
        name="mix_route",
    )(merged, wm_b, x2, mod3, gpost, gpre, w_r, b_r)


DMA_ISSUE_UNROLL = 8


def _dispatch_body(d1_ref, d2_ref, f_ref, xs_hbm, stage, sem, *, n_steps):
    i = pl.program_id(0)
    tm = f_ref.shape[0]
    slot = i % 2

    def drain(s):
        for _ in range(TOP_K):
            pltpu.make_async_copy(stage.at[s], xs_hbm.at[pl.ds(0, tm)], sem.at[s]).wait()

    @pl.when(i >= 2)
    def _():
        drain(slot)

    stage[slot] = f_ref[...]
    base = i * tm

    def body(r, carry):
        t = base + r
        src = stage.at[slot, pl.ds(r, 1)]
        pltpu.make_async_copy(src, xs_hbm.at[pl.ds(d1_ref[t], 1)], sem.at[slot]).start()
        pltpu.make_async_copy(src, xs_hbm.at[pl.ds(d2_ref[t], 1)], sem.at[slot]).start(priority=1)
        return carry

    lax.fori_loop(0, tm, body, 0, unroll=DMA_ISSUE_UNROLL)

    @pl.when(i == n_steps - 1)
    def _():
        drain(slot)
        if n_steps >= 2:
            drain(1 - slot)


def _dispatch(dest1, dest2, f, *, n_rows, tm):
    n_tok, d = f.shape
    n_steps = n_tok // tm
    body = functools.partial(_dispatch_body, n_steps=n_steps)
    grid_spec = pltpu.PrefetchScalarGridSpec(
        num_scalar_prefetch=2,
        grid=(n_steps,),
        in_specs=[pl.BlockSpec((tm, d), lambda i, d1, d2: (i, 0))],
        out_specs=pl.BlockSpec(memory_space=pl.ANY),
        scratch_shapes=[pltpu.VMEM((2, tm, d), f.dtype),
                        pltpu.SemaphoreType.DMA((2,))],
    )
    return pl.pallas_call(
        body,
        grid_spec=grid_spec,
        out_shape=jax.ShapeDtypeStruct((n_rows, d), f.dtype),
        compiler_params=_params(1),
        name="dispatch",
    )(dest1, dest2, f)


def _experts_body(be_ref, nu_ref, valid_ref, first_ref, next_ref, slot_ref,
                  x_ref, wg_hbm, wu_hbm, wd_hbm, o_ref, wg_f, wu_f, wd_f, wsem, wg_b, wu_b, wd_b):
    i = pl.program_id(0)
    weights = ((wg_hbm, wg_f, wg_b), (wu_hbm, wu_f, wu_b), (wd_hbm, wd_f, wd_b))

    def weight_copies(expert, slot):
        return [pltpu.make_async_copy(hbm.at[expert], ring.at[slot], wsem.at[slot, k])
                for k, (hbm, ring, _) in enumerate(weights)]

    @pl.when(i == 0)
    def _():
        for cp in weight_copies(be_ref[0], 0):
            cp.start()

    @pl.when(i < nu_ref[0])
    def _():
        @pl.when(first_ref[i] == 1)
        def _():
            slot = slot_ref[i]
            for cp in weight_copies(be_ref[i], slot):
                cp.wait()
            for _, ring, cast in weights:
                cast[...] = ring[slot].astype(BF16)

            @pl.when(next_ref[i] >= 0)
            def _():
                for cp in weight_copies(next_ref[i], 1 - slot):
                    cp.start()

        row = lax.broadcasted_iota(jnp.int32, x_ref.shape, 0)
        x = jnp.where(row < valid_ref[i], x_ref[...], 0.0).astype(BF16)
        g = jnp.dot(x, wg_b[...], preferred_element_type=F32)
        u = jnp.dot(x, wu_b[...], preferred_element_type=F32)
        hid = (g * jax.nn.sigmoid(g) * u).astype(BF16)
        o_ref[...] = jnp.dot(hid, wd_b[...], preferred_element_type=F32)


def _experts(block_expert, n_used, block_valid, block_first, block_next, block_slot, xs, w_gate, w_up, w_down, *, bm):
    n_blocks = block_expert.shape[0]
    _, d, de = w_gate.shape

    def row_block(i, be, nu, *_):
        return (jnp.minimum(i, nu[0] - 1), 0)

    grid_spec = pltpu.PrefetchScalarGridSpec(
        num_scalar_prefetch=6,
        grid=(n_blocks,),
        in_specs=[pl.BlockSpec((bm, d), row_block),
                  pl.BlockSpec(memory_space=pl.ANY),
                  pl.BlockSpec(memory_space=pl.ANY),
                  pl.BlockSpec(memory_space=pl.ANY)],
        out_specs=pl.BlockSpec((bm, d), row_block),
        scratch_shapes=[pltpu.VMEM((2, d, de), F32),
                        pltpu.VMEM((2, d, de), F32),
                        pltpu.VMEM((2, de, d), F32),
                        pltpu.SemaphoreType.DMA((2, 3)),
                        pltpu.VMEM((d, de), BF16),
                        pltpu.VMEM((d, de), BF16),
                        pltpu.VMEM((de, d), BF16)],
    )
    return pl.pallas_call(
        _experts_body,
        grid_spec=grid_spec,
        out_shape=jax.ShapeDtypeStruct((n_blocks * bm, d), F32),
        compiler_params=_params(1),
        name="experts",
    )(block_expert, n_used, block_valid, block_first, block_next, block_slot, xs, w_gate, w_up, w_down)


def _combine_body(d1_ref, d2_ref, yb_hbm, route_ref, h_ref, mod_ref, g_ref, o_ref, ybuf, sem):
    i = pl.program_id(0)
    n = pl.num_programs(0)
    tm = h_ref.shape[0]

    def start_gather(blk, slot):
        base = blk * tm

        def body(r, carry):
            pltpu.make_async_copy(yb_hbm.at[pl.ds(d1_ref[base + r], 1)], ybuf.at[slot, 0, pl.ds(r, 1)],
                                  sem.at[slot]).start()
            pltpu.make_async_copy(yb_hbm.at[pl.ds(d2_ref[base + r], 1)], ybuf.at[slot, 1, pl.ds(r, 1)],
                                  sem.at[slot]).start(priority=1)
            return carry

        lax.fori_loop(0, tm, body, 0, unroll=DMA_ISSUE_UNROLL)

    def wait_gather(slot):
        for k in range(TOP_K):
            pltpu.make_async_copy(yb_hbm.at[pl.ds(0, tm)], ybuf.at[slot, k], sem.at[slot]).wait()

    @pl.when(i == 0)
    def _():
        start_gather(0, 0)

    @pl.when(i + 1 < n)
    def _():
        start_gather(i + 1, (i + 1) % 2)

    slot = i % 2
    wait_gather(slot)
    y = route_ref[:, 2:3] * ybuf[slot, 0] + route_ref[:, 3:4] * ybuf[slot, 1]
    o_ref[...] = h_ref[...] + mod_ref[0, 5:6, :] * _rms(y, g_ref[...])


def _combine(dest1, dest2, yb, route, h, mod3, gpost, *, tm, seq):
    rows, d = h.shape
    per_batch = seq // tm
    grid_spec = pltpu.PrefetchScalarGridSpec(
        num_scalar_prefetch=2,
        grid=(rows // tm,),
        in_specs=[pl.BlockSpec(memory_space=pl.ANY),
                  pl.BlockSpec((tm, LANES), lambda i, d1, d2: (i, 0)),
                  pl.BlockSpec((tm, d), lambda i, d1, d2: (i, 0)),
                  pl.BlockSpec((1, N_MOD, d), lambda i, d1, d2: (i // per_batch, 0, 0)),
                  pl.BlockSpec((1, d), lambda i, d1, d2: (0, 0))],
        out_specs=pl.BlockSpec((tm, d), lambda i, d1, d2: (i, 0)),
        scratch_shapes=[pltpu.VMEM((2, TOP_K, tm, d), F32),
                        pltpu.SemaphoreType.DMA((2,))],
    )
    return pl.pallas_call(
        _combine_body,
        grid_spec=grid_spec,
        out_shape=jax.ShapeDtypeStruct((rows, d), F32),
        compiler_params=_params(1),
        name="combine",
    )(dest1, dest2, yb, route, h, mod3, gpost)


def _rope_tables(seq):
    rows = seq // GRID_W
    row = jnp.repeat(jnp.arange(rows, dtype=F32), GRID_W)
    col = jnp.tile(jnp.arange(GRID_W, dtype=F32), rows)
    pairs = HEAD_DIM // 4
    freqs = ROPE_THETA ** (-jnp.arange(pairs, dtype=F32) / pairs)
    ang = jnp.concatenate([row[:, None] * freqs, col[:, None] * freqs], axis=-1)
    cos, sin = jnp.cos(ang), jnp.sin(ang)
    return jnp.concatenate([cos, cos], axis=-1), jnp.concatenate([-sin, sin], axis=-1)


def _tile(limit, size):
    t = min(limit, size)
    assert size % t == 0, (limit, size)
    return t


def kernel(x, c, ctx, c_ctx, w_ada, b_ada, g_pre_mix, w_in, q_norm, k_norm, conv_w, w_attn_out, w_conv_out,
           w_mix_out, g_post_mix, g_pre_ffn, w_router_group, b_router_group, w_router_expert, b_router_expert,
           w_gate, w_up, w_down, g_post_ffn):
    assert w_ada.shape[0] == 1, "single-layer problem"
    batch, seq, d = x.shape
    n_ctx = ctx.shape[1]
    attn_width = w_attn_out.shape[1]
    conv_width = conv_w.shape[-1]
    in_width = w_in.shape[-1]
    kv_width = (in_width - attn_width - 3 * conv_width - 2 * d) // 2
    n_kv = kv_width // HEAD_DIM
    group = attn_width // kv_width
    conv_start = attn_width + 2 * kv_width
    gate_start = conv_start + 3 * conv_width
    n_groups, per_group = w_router_expert.shape[2], w_router_expert.shape[3]
    n_experts = n_groups * per_group
    assert n_groups + n_experts <= LANES and batch + 1 <= 8
    n_tok = batch * seq

    cc = jnp.concatenate([c, c_ctx[None, :], jnp.zeros((8 - batch - 1, d), F32)], axis=0)
    mod3 = _adaln(cc, w_ada[0], b_ada[0]).reshape(8, N_MOD, d)

    w_in_b = w_in[0].astype(BF16)
    cos2, sin2 = _rope_tables(seq)
    tn = _tile(512, kv_width)
    tm_in = _tile(512, seq)
    x2 = x.reshape(n_tok, d)
    common = dict(tn=tn, attn_width=attn_width, kv_width=kv_width, gate_start_col=gate_start)
    per_batch_in = seq // tm_in
    n_sections = 2
    assert in_width % (n_sections * tn) == 0 and attn_width % (2 * kv_width) == 0
    p_all = _inproj(x2, mod3, g_pre_mix, w_in_b, cos2, sin2, q_norm, k_norm, tm=tm_in,
                    sec_width=in_width // n_sections, sec_off=0, n_sections=n_sections,
                    mod_row=lambda i: i // per_batch_in, pos_blocks=per_batch_in, use_rope=True, **common)
    tm_ctx = _tile(512, n_ctx)
    p_ctx = _inproj(ctx.reshape(batch * n_ctx, d), mod3, g_pre_mix, w_in_b, cos2[:tm_ctx], sin2[:tm_ctx],
                    q_norm, k_norm, tm=tm_ctx, sec_width=2 * kv_width, sec_off=attn_width // (2 * kv_width),
                    n_sections=1, mod_row=lambda i: batch, pos_blocks=1, use_rope=False, **common)

    attn = _attention(p_all, p_ctx, batch=batch, seq=seq, n_ctx=n_ctx, n_kv=n_kv, group=group,
                      tq=_tile(128, seq))

    merged = _merge(attn, p_all, conv_w[0], w_attn_out[0].astype(BF16), w_conv_out[0].astype(BF16),
                    tm=_tile(256, seq), seq=seq, conv_width=conv_width, d=d,
                    conv_start_col=conv_start, gate_start_col=gate_start)

    pad = LANES - n_groups - n_experts
    w_r = jnp.concatenate([w_router_group[0], w_router_expert[0].reshape(d, n_experts), jnp.zeros((d, pad), F32)], axis=1)
    b_r = jnp.concatenate([b_router_group[0], b_router_expert[0].reshape(n_experts), jnp.zeros((pad,), F32)])[None, :]
    tm_tok = _tile(256, seq)
    h, f, route, counts = _mix_route(merged, w_mix_out[0].astype(BF16), x2, mod3, g_post_mix, g_pre_ffn, w_r, b_r,
                                     tm=_tile(512, seq), seq=seq, n_groups=n_groups, per_group=per_group)

    bm = _tile(256, n_tok)
    n_blocks = n_tok * TOP_K // bm + n_experts
    cnt = counts[0, :n_experts].astype(jnp.int32)
    padded = (cnt + bm - 1) // bm * bm
    pend = jnp.cumsum(padded)
    pstart = pend - padded
    e1, e2 = route[:, 0].astype(jnp.int32), route[:, 1].astype(jnp.int32)
    expert_ids = jnp.arange(n_experts, dtype=jnp.int32)[None, :]

    def segment_start(e):
        return jnp.sum(jnp.where(e[:, None] == expert_ids, pstart[None, :], 0), axis=1)

    dest1 = segment_start(e1) + route[:, 4].astype(jnp.int32)
    dest2 = segment_start(e2) + route[:, 5].astype(jnp.int32)
    n_used = (pend[-1] // bm).astype(jnp.int32)
    blk_row = jnp.minimum(jnp.arange(n_blocks, dtype=jnp.int32), n_used - 1) * bm
    blk_exp = jnp.sum((pend[None, :] <= blk_row[:, None]).astype(jnp.int32), axis=1)
    blk_exp = jnp.minimum(blk_exp, n_experts - 1)
    def per_block(table):
        return jnp.sum(jnp.where(blk_exp[:, None] == expert_ids, table[None, :], 0), axis=1)

    blk_valid = jnp.clip(per_block(pstart + cnt) - blk_row, 0, bm)
    blk_first = (blk_row == per_block(pstart)).astype(jnp.int32)
    used = cnt > 0
    later = jnp.where(used[None, :] & (expert_ids > expert_ids.T), expert_ids, n_experts)
    next_used = jnp.min(later, axis=1)
    blk_next = per_block(jnp.where(next_used < n_experts, next_used, -1))
    blk_slot = per_block((jnp.cumsum(used.astype(jnp.int32)) - 1) % 2)

    xs = _dispatch(dest1, dest2, f, n_rows=n_blocks * bm, tm=tm_tok)
    yb = _experts(blk_exp, n_used.reshape(1), blk_valid, blk_first, blk_next, blk_slot, xs,
                  w_gate[0], w_up[0], w_down[0], bm=bm)
    out = _combine(dest1, dest2, yb, route, h, mod3, g_post_ffn, tm=tm_tok, seq=seq)
    return out.reshape(batch, seq, d)
```

```python
import functools

import jax
import jax.numpy as jnp
from jax import lax
from jax.experimental import pallas as pl
from jax.experimental.pallas import tpu as pltpu

HEAD_DIM = 128
GRID_W = 64
ROPE_THETA = 10000.0
EPS = 1e-6
TOP_K = 2
N_MOD = 6
LANES = 128
BF16_SUBLANES = 16
VMEM_LIMIT_BYTES = 56 * 1024 * 1024
NEG_BIG = -1e30
LOG2_E = 1.4426950408889634

F32 = jnp.float32
BF16 = jnp.bfloat16


def _params(n_grid_axes, flags=None):
    return pltpu.CompilerParams(dimension_semantics=("arbitrary",) * n_grid_axes,
                                vmem_limit_bytes=VMEM_LIMIT_BYTES, flags=flags)


def _rms(t, gain):
    ms = jnp.mean(t * t, axis=-1, keepdims=True)
    return t * lax.rsqrt(ms + EPS) * gain


def _adaln_body(c_ref, w_ref, b_ref, o_ref):
    c = c_ref[...]
    s = (c * jax.nn.sigmoid(c)).astype(BF16)
    o_ref[...] = jnp.dot(s, w_ref[...].astype(BF16), preferred_element_type=F32) + b_ref[...]


def _adaln(cc, w, b):
    rows, d = cc.shape
    n = w.shape[1]
    tn = min(1024, n)
    return pl.pallas_call(
        _adaln_body,
        grid=(n // tn,),
        in_specs=[pl.BlockSpec((rows, d), lambda j: (0, 0)),
                  pl.BlockSpec((d, tn), lambda j: (0, j)),
                  pl.BlockSpec((1, tn), lambda j: (0, j))],
        out_specs=pl.BlockSpec((rows, tn), lambda j: (0, j)),
        out_shape=jax.ShapeDtypeStruct((rows, n), F32),
        compiler_params=_params(1),
        name="adaln",
    )(cc, w, b.reshape(1, n))


Q_COLS, K_COLS, PLAIN_COLS, GATE_COLS = "q", "k", "plain", "gate"


def _inproj_body(x_ref, mod_ref, g_ref, w_ref, cos_ref, sin_ref, qn_ref, kn_ref, o_ref, *,
                 sections, tn, use_rope, q_scale):
    sec = pl.program_id(0)
    tm = x_ref.shape[0]
    half = tm // 2

    def normalised(r0):
        xn = _rms(x_ref[r0:r0 + half, :], g_ref[...])
        return (xn * (1.0 + mod_ref[0, 1:2, :]) + mod_ref[0, 0:1, :]).astype(BF16)

    def normed_heads(acc, r0, col0, gain, out_scale):
        for hh in range(tn // HEAD_DIM):
            t = _rms(acc[:, hh * HEAD_DIM:(hh + 1) * HEAD_DIM], gain)
            if use_rope:
                t = t * cos_ref[r0:r0 + half, :] + pltpu.roll(t, HEAD_DIM // 2, 1) * sin_ref[r0:r0 + half, :]
            if out_scale != 1.0:
                t = t * out_scale
            o_ref[r0:r0 + half, col0 + hh * HEAD_DIM:col0 + (hh + 1) * HEAD_DIM] = t.astype(o_ref.dtype)

    def column_tile(a, r0, col0, kind):
        acc = jnp.dot(a, w_ref[:, col0:col0 + tn], preferred_element_type=F32)
        if kind == Q_COLS:
            normed_heads(acc, r0, col0, qn_ref[...], q_scale)
        elif kind == K_COLS:
            normed_heads(acc, r0, col0, kn_ref[...], 1.0)
        elif kind == PLAIN_COLS:
            o_ref[r0:r0 + half, col0:col0 + tn] = acc.astype(o_ref.dtype)
        else:
            o_ref[r0:r0 + half, col0:col0 + tn] = jax.nn.sigmoid(acc).astype(o_ref.dtype)

    def section(kinds):
        a_lo = normalised(0)
        column_tile(a_lo, 0, 0, kinds[0])
        a_hi = normalised(half)
        column_tile(a_hi, half, 0, kinds[0])
        for t, kind in enumerate(kinds[1:], start=1):
            column_tile(a_lo, 0, t * tn, kind)
            column_tile(a_hi, half, t * tn, kind)

    for s_idx, kinds in enumerate(sections):
        pl.when(sec == s_idx)(functools.partial(section, kinds))


def _inproj(x2, mod3, g, w_b, cos2, sin2, qn, kn, *, tm, tn, sec_width, sec_off, n_sections, mod_row, pos_blocks,
            attn_width, kv_width, gate_start_col, use_rope):
    rows, d = x2.shape

    def kind_of(col):
        if col < attn_width:
            return Q_COLS
        if col < attn_width + kv_width:
            return K_COLS
        return PLAIN_COLS if col < gate_start_col else GATE_COLS

    sections = tuple(tuple(kind_of((sec_off + s) * sec_width + t * tn) for t in range(sec_width // tn))
                     for s in range(n_sections))
    body = functools.partial(_inproj_body, sections=sections, tn=tn, use_rope=use_rope,
                             q_scale=HEAD_DIM ** -0.5 * LOG2_E)
    return pl.pallas_call(
        body,
        grid=(n_sections, rows // tm),
        in_specs=[pl.BlockSpec((tm, d), lambda s, i: (i, 0)),
                  pl.BlockSpec((1, N_MOD, d), lambda s, i: (mod_row(i), 0, 0)),
                  pl.BlockSpec((1, d), lambda s, i: (0, 0)),
                  pl.BlockSpec((d, sec_width), lambda s, i: (0, s + sec_off), pipeline_mode=pl.Buffered(1)),
                  pl.BlockSpec((tm, HEAD_DIM), lambda s, i: (i % pos_blocks, 0)),
                  pl.BlockSpec((tm, HEAD_DIM), lambda s, i: (i % pos_blocks, 0)),
                  pl.BlockSpec((1, HEAD_DIM), lambda s, i: (0, 0)),
                  pl.BlockSpec((1, HEAD_DIM), lambda s, i: (0, 0))],
        out_specs=pl.BlockSpec((tm, sec_width), lambda s, i: (i, s)),
        out_shape=jax.ShapeDtypeStruct((rows, n_sections * sec_width), BF16),
        compiler_params=_params(2),
        name="inproj_rope" if use_rope else "inproj_ctx",
    )(x2, mod3, g, w_b, cos2, sin2, qn, kn)


ATTN_KEY_CHUNK = 256


def _attn_body(q_ref, k_ref, v_ref, kc_ref, vc_ref, o_ref, s0_scr, m0_scr, s1_scr, m1_scr, vt_scr, *, group, tq):
    seq = q_ref.shape[0]
    q_tiles = seq // tq
    slabs = ((s0_scr, m0_scr), (s1_scr, m1_scr))
    vt_scr[:, :seq] = v_ref[...].astype(F32).T.astype(BF16)
    vt_scr[:, seq:] = vc_ref[...].astype(F32).T.astype(BF16)

    def rows(t):
        start = t * tq
        return pl.ds(start if isinstance(start, int) else pl.multiple_of(start, tq), tq)

    n_ctx = kc_ref.shape[0]
    chunk = min(ATTN_KEY_CHUNK, seq)
    key_chunks = [(k_ref, c0, c0, chunk) for c0 in range(0, seq, chunk)] + [(kc_ref, 0, seq, n_ctx)]

    def step(t, score_slab, finish_slab):
        if score_slab is not None:
            q = jnp.concatenate([q_ref[rows(t), g * HEAD_DIM:(g + 1) * HEAD_DIM] for g in range(group)], axis=0)
            m_run = None
        if finish_slab is not None:
            m_prev = finish_slab[1][...]
            l = 0.0
            acc = 0.0
        for keys, k0, s0, size in key_chunks:
            if score_slab is not None:
                s = lax.dot_general(keys[k0:k0 + size, :], q, (((1,), (1,)), ((), ())),
                                    preferred_element_type=F32)
                score_slab[0][s0:s0 + size] = s
                m_chunk = jnp.max(s, axis=0, keepdims=True)
                m_run = m_chunk if m_run is None else jnp.maximum(m_run, m_chunk)
            if finish_slab is not None:
                p = jnp.exp2(finish_slab[0][s0:s0 + size] - m_prev)
                l = l + jnp.sum(p, axis=0, keepdims=True)
                acc = acc + jnp.dot(vt_scr[:, s0:s0 + size], p.astype(BF16), preferred_element_type=F32)
        if score_slab is not None:
            score_slab[1][...] = m_run
        if finish_slab is not None:
            out = acc * (1.0 / l)
            for g in range(group):
                o_ref[rows(t - 1), g * HEAD_DIM:(g + 1) * HEAD_DIM] = (
                    out[:, g * tq:(g + 1) * tq].T.astype(o_ref.dtype))

    step(0, slabs[0], None)
    n_pairs = (q_tiles - 1) // 2

    def pair(j, carry):
        t = 1 + 2 * j
        step(t, slabs[1], slabs[0])
        step(t + 1, slabs[0], slabs[1])
        return carry

    if n_pairs > 0:
        lax.fori_loop(0, n_pairs, pair, 0)
    last = q_tiles - 1
    if last % 2:
        step(last, slabs[1], slabs[0])
    step(q_tiles, None, slabs[last % 2])


def _attention(p_all, p_ctx, *, batch, seq, n_ctx, n_kv, group, tq):
    gw = group * HEAD_DIM
    body = functools.partial(_attn_body, group=group, tq=tq)
    n_keys = seq + n_ctx
    k_col = n_kv * group
    v_col = k_col + n_kv
    return pl.pallas_call(
        body,
        grid=(batch, n_kv),
        in_specs=[pl.BlockSpec((seq, gw), lambda b, h: (b, h)),
                  pl.BlockSpec((seq, HEAD_DIM), lambda b, h: (b, k_col + h)),
                  pl.BlockSpec((seq, HEAD_DIM), lambda b, h: (b, v_col + h)),
                  pl.BlockSpec((n_ctx, HEAD_DIM), lambda b, h: (b, h)),
                  pl.BlockSpec((n_ctx, HEAD_DIM), lambda b, h: (b, n_kv + h))],
        out_specs=pl.BlockSpec((seq, gw), lambda b, h: (b, h)),
        out_shape=jax.ShapeDtypeStruct((batch * seq, n_kv * gw), BF16),
        scratch_shapes=[pltpu.VMEM((n_keys, group * tq), F32), pltpu.VMEM((1, group * tq), F32)] * 2
        + [pltpu.VMEM((HEAD_DIM, n_keys), BF16)],
        compiler_params=_params(2),
        name="attention",
    )(p_all, p_all, p_all, p_ctx, p_ctx)


MERGE_CHUNKS = 4


def _merge_body(attn_ref, pb_ref, pc_ref, px_ref, pcp_ref, pxp_ref, pcn_ref, pxn_ref, ga_ref, gc_ref,
                cw_ref, wa_ref, wc_ref, o_ref, *, tiles_per_seq):
    i = pl.program_id(0)
    tm = attn_ref.shape[0]
    attn = attn_ref[...]
    width = o_ref.shape[1] // MERGE_CHUNKS
    y_attn0 = jnp.dot(attn, wa_ref[:, :width], preferred_element_type=F32)

    u = pc_ref[...].astype(F32) * px_ref[...].astype(F32)
    last = BF16_SUBLANES - 1
    u_before = pcp_ref[last:last + 1, :].astype(F32) * pxp_ref[last:last + 1, :].astype(F32)
    u_after = pcn_ref[0:1, :].astype(F32) * pxn_ref[0:1, :].astype(F32)
    pos = i % tiles_per_seq
    u_before = jnp.where(pos == 0, 0.0, u_before)
    u_after = jnp.where(pos == tiles_per_seq - 1, 0.0, u_after)
    row = lax.broadcasted_iota(jnp.int32, u.shape, 0)
    u_prev = jnp.where(row == 0, u_before, pltpu.roll(u, 1, 0))
    u_next = jnp.where(row == tm - 1, u_after, pltpu.roll(u, tm - 1, 0))
    conv = cw_ref[0:1, :] * u_prev + cw_ref[1:2, :] * u + cw_ref[2:3, :] * u_next
    z = (pb_ref[...].astype(F32) * conv).astype(BF16)

    for c in range(MERGE_CHUNKS):
        cols = slice(c * width, (c + 1) * width)
        y_attn = y_attn0 if c == 0 else jnp.dot(attn, wa_ref[:, cols], preferred_element_type=F32)
        y_conv = jnp.dot(z, wc_ref[:, cols], preferred_element_type=F32)
        merged = ga_ref[:, cols].astype(F32) * y_attn + gc_ref[:, cols].astype(F32) * y_conv
        o_ref[:, cols] = merged.astype(o_ref.dtype)


def _merge(attn, p_all, conv_w, wa_b, wc_b, *, tm, seq, conv_width, d, conv_start_col, gate_start_col):
    rows, attn_width = attn.shape
    hb = BF16_SUBLANES
    cb = conv_start_col // conv_width
    gb = gate_start_col // d
    n_hblocks = rows // hb
    per = tm // hb

    def prev_map(col):
        return lambda i: (jnp.maximum(i * per - 1, 0), col)

    def next_map(col):
        return lambda i: (jnp.minimum((i + 1) * per, n_hblocks - 1), col)

    body = functools.partial(_merge_body, tiles_per_seq=seq // tm)
    resident = dict(pipeline_mode=pl.Buffered(1))
    return pl.pallas_call(
        body,
        grid=(rows // tm,),
        in_specs=[pl.BlockSpec((tm, attn_width), lambda i: (i, 0)),
                  pl.BlockSpec((tm, conv_width), lambda i: (i, cb)),
                  pl.BlockSpec((tm, conv_width), lambda i: (i, cb + 1)),
                  pl.BlockSpec((tm, conv_width), lambda i: (i, cb + 2)),
                  pl.BlockSpec((hb, conv_width), prev_map(cb + 1)),
                  pl.BlockSpec((hb, conv_width), prev_map(cb + 2)),
                  pl.BlockSpec((hb, conv_width), next_map(cb + 1)),
                  pl.BlockSpec((hb, conv_width), next_map(cb + 2)),
                  pl.BlockSpec((tm, d), lambda i: (i, gb)),
                  pl.BlockSpec((tm, d), lambda i: (i, gb + 1)),
                  pl.BlockSpec(conv_w.shape, lambda i: (0, 0)),
                  pl.BlockSpec(wa_b.shape, lambda i: (0, 0), **resident),
                  pl.BlockSpec(wc_b.shape, lambda i: (0, 0), **resident)],
        out_specs=pl.BlockSpec((tm, d), lambda i: (i, 0)),
        out_shape=jax.ShapeDtypeStruct((rows, d), BF16),
        compiler_params=_params(1),
        name="merge",
    )(attn, p_all, p_all, p_all, p_all, p_all, p_all, p_all, p_all, p_all, conv_w, wa_b, wc_b)


def _split_bf16(t):
    hi = t.astype(BF16)
    lo = (t - hi.astype(F32)).astype(BF16)
    return hi, lo


def _mix_route_body(m_ref, wm_ref, x_ref, mod_ref, gpost_ref, gpre_ref, wr_ref, br_ref,
                    h_ref, f_ref, route_ref, cnt_ref, carry_scr, f0_scr, f1_scr, *, n_groups, per_group, n_tiles):
    i = pl.program_id(0)
    f_slots = (f0_scr, f1_scr)

    def project(f_scr, stages=()):
        m = m_ref[...]
        stages = iter(stages)
        width = wm_ref.shape[1] // ROUTE_STAGES
        ys = []
        for c in range(ROUTE_STAGES):
            ys.append(jnp.dot(m, wm_ref[:, c * width:(c + 1) * width], preferred_element_type=F32))
            next(stages, None)
        y = jnp.concatenate(ys, axis=1)
        h = x_ref[...] + mod_ref[0, 2:3, :] * _rms(y, gpost_ref[...])
        h_ref[...] = h
        f = _rms(h, gpre_ref[...]) * (1.0 + mod_ref[0, 4:5, :]) + mod_ref[0, 3:4, :]
        f_ref[...] = f
        f_scr[...] = f

    route = functools.partial(_route_stages, wr_ref, br_ref, route_ref, cnt_ref, carry_scr,
                              n_groups=n_groups, per_group=per_group)

    @pl.when(i == 0)
    def _():
        carry_scr[...] = jnp.zeros_like(carry_scr)
        project(f_slots[0])

    for parity in range(2):
        @pl.when((i % 2 == parity) & (i > 0) & (i < n_tiles))
        def _(parity=parity):
            project(f_slots[parity], route(f_slots[1 - parity]))

    @pl.when(i == n_tiles)
    def _():
        for _ in route(f_slots[(n_tiles - 1) % 2]):
            pass


ROUTE_STAGES = 4


def _route_stages(wr_ref, br_ref, route_ref, cnt_ref, carry_scr, f_scr, *, n_groups, per_group):
    f = f_scr[...]
    tm = f.shape[0]
    f_hi, f_lo = _split_bf16(f)
    w_hi, w_lo = _split_bf16(wr_ref[...])
    logits = (jnp.dot(f_hi, w_hi, preferred_element_type=F32)
              + jnp.dot(f_hi, w_lo, preferred_element_type=F32)
              + jnp.dot(f_lo, w_hi, preferred_element_type=F32)) + br_ref[...]
    yield

    lane = lax.broadcasted_iota(jnp.int32, logits.shape, 1)

    def first_argmax(vals):
        mx = jnp.max(vals, axis=-1, keepdims=True)
        idx = jnp.min(jnp.where(vals == mx, lane, LANES), axis=-1, keepdims=True)
        return mx, idx

    is_group = lane < n_groups
    g_max, g_idx = first_argmax(jnp.where(is_group, logits, NEG_BIG))
    g_den = jnp.sum(jnp.where(is_group, jnp.exp(logits - g_max), 0.0), axis=-1, keepdims=True)
    grp_p = 1.0 / g_den
    yield
    lo_lane = n_groups + g_idx * per_group
    e_logits = jnp.where((lane >= lo_lane) & (lane < lo_lane + per_group), logits, NEG_BIG)
    l1, i1 = first_argmax(e_logits)
    l2, i2 = first_argmax(jnp.where(lane == i1, NEG_BIG, e_logits))
    r = jnp.exp(l2 - l1)
    gate1 = grp_p / (1.0 + r)
    gate2 = grp_p * r / (1.0 + r)
    e1 = i1 - n_groups
    e2 = i2 - n_groups
    yield

    onehot = ((lane == e1) | (lane == e2)).astype(BF16)
    r_i = lax.broadcasted_iota(jnp.int32, (tm, tm), 0)
    c_i = lax.broadcasted_iota(jnp.int32, (tm, tm), 1)
    before = (c_i < r_i).astype(BF16)
    seen = jnp.dot(before, onehot, preferred_element_type=F32) + carry_scr[0:1, :]
    rank1 = jnp.sum(jnp.where(lane == e1, seen, 0.0), axis=-1, keepdims=True)
    rank2 = jnp.sum(jnp.where(lane == e2, seen, 0.0), axis=-1, keepdims=True)
    carry_scr[...] = carry_scr[...] + jnp.sum(onehot.astype(F32), axis=0, keepdims=True)
    cnt_ref[...] = carry_scr[...]

    route = jnp.where(lane == 0, e1.astype(F32), 0.0)
    route = jnp.where(lane == 1, e2.astype(F32), route)
    route = jnp.where(lane == 2, gate1, route)
    route = jnp.where(lane == 3, gate2, route)
    route = jnp.where(lane == 4, rank1, route)
    route = jnp.where(lane == 5, rank2, route)
    route_ref[...] = route
    yield


def _mix_route(merged, wm_b, x2, mod3, gpost, gpre, w_r, b_r, *, tm, seq, n_groups, per_group):
    rows, d = x2.shape
    per_batch = seq // tm
    n_tiles = rows // tm
    body = functools.partial(_mix_route_body, n_groups=n_groups, per_group=per_group, n_tiles=n_tiles)

    def tile(i):
        return jnp.minimum(i, n_tiles - 1)

    def routed(i):
        return jnp.maximum(i - 1, 0)

    return pl.pallas_call(
        body,
        grid=(n_tiles + 1,),
        in_specs=[pl.BlockSpec((tm, d), lambda i: (tile(i), 0)),
                  pl.BlockSpec(wm_b.shape, lambda i: (0, 0), pipeline_mode=pl.Buffered(1)),
                  pl.BlockSpec((tm, d), lambda i: (tile(i), 0)),
                  pl.BlockSpec((1, N_MOD, d), lambda i: (tile(i) // per_batch, 0, 0)),
                  pl.BlockSpec((1, d), lambda i: (0, 0)),
                  pl.BlockSpec((1, d), lambda i: (0, 0)),
                  pl.BlockSpec((d, LANES), lambda i: (0, 0)),
                  pl.BlockSpec((1, LANES), lambda i: (0, 0))],
        out_specs=[pl.BlockSpec((tm, d), lambda i: (tile(i), 0)),
                   pl.BlockSpec((tm, d), lambda i: (tile(i), 0)),
                   pl.BlockSpec((tm, LANES), lambda i: (routed(i), 0)),
                   pl.BlockSpec((8, LANES), lambda i: (0, 0))],
        out_shape=[jax.ShapeDtypeStruct((rows, d), F32),
                   jax.ShapeDtypeStruct((rows, d), F32),
                   jax.ShapeDtypeStruct((rows, LANES), F32),
                   jax.ShapeDtypeStruct((8, LANES), F32)],
        scratch_shapes=[pltpu.VMEM((8, LANES), F32), pltpu.VMEM((tm, d), F32), pltpu.VMEM((tm, d), F32)],
        compiler_params=_params(1),
        name="mix_route",
    )(merged, wm_b, x2, mod3, gpost, gpre, w_r, b_r)


DMA_ISSUE_UNROLL = 8


def _dispatch_body(d1_ref, d2_ref, f_ref, xs_hbm, stage, sem, *, n_steps):
    i = pl.program_id(0)
    tm = f_ref.shape[0]
    slot = i % 2

    def drain(s):
        for _ in range(TOP_K):
            pltpu.make_async_copy(stage.at[s], xs_hbm.at[pl.ds(0, tm)], sem.at[s]).wait()

    @pl.when(i >= 2)
    def _():
        drain(slot)

    stage[slot] = f_ref[...]
    base = i * tm

    def body(r, carry):
        t = base + r
        src = stage.at[slot, pl.ds(r, 1)]
        pltpu.make_async_copy(src, xs_hbm.at[pl.ds(d1_ref[t], 1)], sem.at[slot]).start()
        pltpu.make_async_copy(src, xs_hbm.at[pl.ds(d2_ref[t], 1)], sem.at[slot]).start(priority=1)
        return carry

    lax.fori_loop(0, tm, body, 0, unroll=DMA_ISSUE_UNROLL)

    @pl.when(i == n_steps - 1)
    def _():
        drain(slot)
        if n_steps >= 2:
            drain(1 - slot)


def _dispatch(dest1, dest2, f, *, n_rows, tm):
    n_tok, d = f.shape
    n_steps = n_tok // tm
    body = functools.partial(_dispatch_body, n_steps=n_steps)
    grid_spec = pltpu.PrefetchScalarGridSpec(
        num_scalar_prefetch=2,
        grid=(n_steps,),
        in_specs=[pl.BlockSpec((tm, d), lambda i, d1, d2: (i, 0))],
        out_specs=pl.BlockSpec(memory_space=pl.ANY),
        scratch_shapes=[pltpu.VMEM((2, tm, d), f.dtype),
                        pltpu.SemaphoreType.DMA((2,))],
    )
    return pl.pallas_call(
        body,
        grid_spec=grid_spec,
        out_shape=jax.ShapeDtypeStruct((n_rows, d), f.dtype),
        compiler_params=_params(1),
        name="dispatch",
    )(dest1, dest2, f)


def _experts_body(be_ref, nu_ref, valid_ref, first_ref, next_ref, slot_ref,
                  x_ref, wg_hbm, wu_hbm, wd_hbm, o_ref, wg_f, wu_f, wd_f, wsem, wg_b, wu_b, wd_b):
    i = pl.program_id(0)
    weights = ((wg_hbm, wg_f, wg_b), (wu_hbm, wu_f, wu_b), (wd_hbm, wd_f, wd_b))

    def weight_copies(expert, slot):
        return [pltpu.make_async_copy(hbm.at[expert], ring.at[slot], wsem.at[slot, k])
                for k, (hbm, ring, _) in enumerate(weights)]

    @pl.when(i == 0)
    def _():
        for cp in weight_copies(be_ref[0], 0):
            cp.start()

    @pl.when(i < nu_ref[0])
    def _():
        @pl.when(first_ref[i] == 1)
        def _():
            slot = slot_ref[i]
            for cp in weight_copies(be_ref[i], slot):
                cp.wait()
            for _, ring, cast in weights:
                cast[...] = ring[slot].astype(BF16)

            @pl.when(next_ref[i] >= 0)
            def _():
                for cp in weight_copies(next_ref[i], 1 - slot):
                    cp.start()

        row = lax.broadcasted_iota(jnp.int32, x_ref.shape, 0)
        x = jnp.where(row < valid_ref[i], x_ref[...], 0.0).astype(BF16)
        g = jnp.dot(x, wg_b[...], preferred_element_type=F32)
        u = jnp.dot(x, wu_b[...], preferred_element_type=F32)
        hid = (g * jax.nn.sigmoid(g) * u).astype(BF16)
        o_ref[...] = jnp.dot(hid, wd_b[...], preferred_element_type=F32)


def _experts(block_expert, n_used, block_valid, block_first, block_next, block_slot, xs, w_gate, w_up, w_down, *, bm):
    n_blocks = block_expert.shape[0]
    _, d, de = w_gate.shape

    def row_block(i, be, nu, *_):
        return (jnp.minimum(i, nu[0] - 1), 0)

    grid_spec = pltpu.PrefetchScalarGridSpec(
        num_scalar_prefetch=6,
        grid=(n_blocks,),
        in_specs=[pl.BlockSpec((bm, d), row_block),
                  pl.BlockSpec(memory_space=pl.ANY),
                  pl.BlockSpec(memory_space=pl.ANY),
                  pl.BlockSpec(memory_space=pl.ANY)],
        out_specs=pl.BlockSpec((bm, d), row_block),
        scratch_shapes=[pltpu.VMEM((2, d, de), F32),
                        pltpu.VMEM((2, d, de), F32),
                        pltpu.VMEM((2, de, d), F32),
                        pltpu.SemaphoreType.DMA((2, 3)),
                        pltpu.VMEM((d, de), BF16),
                        pltpu.VMEM((d, de), BF16),
                        pltpu.VMEM((de, d), BF16)],
    )
    return pl.pallas_call(
        _experts_body,
        grid_spec=grid_spec,
        out_shape=jax.ShapeDtypeStruct((n_blocks * bm, d), F32),
        compiler_params=_params(1),
        name="experts",
    )(block_expert, n_used, block_valid, block_first, block_next, block_slot, xs, w_gate, w_up, w_down)


def _combine_body(d1_ref, d2_ref, yb_hbm, route_ref, h_ref, mod_ref, g_ref, o_ref, ybuf, sem):
    i = pl.program_id(0)
    n = pl.num_programs(0)
    tm = h_ref.shape[0]

    def start_gather(blk, slot):
        base = blk * tm

        def body(r, carry):
            pltpu.make_async_copy(yb_hbm.at[pl.ds(d1_ref[base + r], 1)], ybuf.at[slot, 0, pl.ds(r, 1)],
                                  sem.at[slot]).start()
            pltpu.make_async_copy(yb_hbm.at[pl.ds(d2_ref[base + r], 1)], ybuf.at[slot, 1, pl.ds(r, 1)],
                                  sem.at[slot]).start(priority=1)
            return carry

        lax.fori_loop(0, tm, body, 0, unroll=DMA_ISSUE_UNROLL)

    def wait_gather(slot):
        for k in range(TOP_K):
            pltpu.make_async_copy(yb_hbm.at[pl.ds(0, tm)], ybuf.at[slot, k], sem.at[slot]).wait()

    @pl.when(i == 0)
    def _():
        start_gather(0, 0)

    @pl.when(i + 1 < n)
    def _():
        start_gather(i + 1, (i + 1) % 2)

    slot = i % 2
    wait_gather(slot)
    y = route_ref[:, 2:3] * ybuf[slot, 0] + route_ref[:, 3:4] * ybuf[slot, 1]
    o_ref[...] = h_ref[...] + mod_ref[0, 5:6, :] * _rms(y, g_ref[...])


def _combine(dest1, dest2, yb, route, h, mod3, gpost, *, tm, seq):
    rows, d = h.shape
    per_batch = seq // tm
    grid_spec = pltpu.PrefetchScalarGridSpec(
        num_scalar_prefetch=2,
        grid=(rows // tm,),
        in_specs=[pl.BlockSpec(memory_space=pl.ANY),
                  pl.BlockSpec((tm, LANES), lambda i, d1, d2: (i, 0)),
                  pl.BlockSpec((tm, d), lambda i, d1, d2: (i, 0)),
                  pl.BlockSpec((1, N_MOD, d), lambda i, d1, d2: (i // per_batch, 0, 0)),
                  pl.BlockSpec((1, d), lambda i, d1, d2: (0, 0))],
        out_specs=pl.BlockSpec((tm, d), lambda i, d1, d2: (i, 0)),
        scratch_shapes=[pltpu.VMEM((2, TOP_K, tm, d), F32),
                        pltpu.SemaphoreType.DMA((2,))],
    )
    return pl.pallas_call(
        _combine_body,
        grid_spec=grid_spec,
        out_shape=jax.ShapeDtypeStruct((rows, d), F32),
        compiler_params=_params(1),
        name="combine",
    )(dest1, dest2, yb, route, h, mod3, gpost)


def _rope_tables(seq):
    rows = seq // GRID_W
    row = jnp.repeat(jnp.arange(rows, dtype=F32), GRID_W)
    col = jnp.tile(jnp.arange(GRID_W, dtype=F32), rows)
    pairs = HEAD_DIM // 4
    freqs = ROPE_THETA ** (-jnp.arange(pairs, dtype=F32) / pairs)
    ang = jnp.concatenate([row[:, None] * freqs, col[:, None] * freqs], axis=-1)
    cos, sin = jnp.cos(ang), jnp.sin(ang)
    return jnp.concatenate([cos, cos], axis=-1), jnp.concatenate([-sin, sin], axis=-1)


def _tile(limit, size):
    t = min(limit, size)
    assert size % t == 0, (limit, size)
    return t


def kernel(x, c, ctx, c_ctx, w_ada, b_ada, g_pre_mix, w_in, q_norm, k_norm, conv_w, w_attn_out, w_conv_out,
           w_mix_out, g_post_mix, g_pre_ffn, w_router_group, b_router_group, w_router_expert, b_router_expert,
           w_gate, w_up, w_down, g_post_ffn):
    assert w_ada.shape[0] == 1, "single-layer problem"
    batch, seq, d = x.shape
    n_ctx = ctx.shape[1]
    attn_width = w_attn_out.shape[1]
    conv_width = conv_w.shape[-1]
    in_width = w_in.shape[-1]
    kv_width = (in_width - attn_width - 3 * conv_width - 2 * d) // 2
    n_kv = kv_width // HEAD_DIM
    group = attn_width // kv_width
    conv_start = attn_width + 2 * kv_width
    gate_start = conv_start + 3 * conv_width
    n_groups, per_group = w_router_expert.shape[2], w_router_expert.shape[3]
    n_experts = n_groups * per_group
    assert n_groups + n_experts <= LANES and batch + 1 <= 8
    n_tok = batch * seq

    cc = jnp.concatenate([c, c_ctx[None, :], jnp.zeros((8 - batch - 1, d), F32)], axis=0)
    mod3 = _adaln(cc, w_ada[0], b_ada[0]).reshape(8, N_MOD, d)

    w_in_b = w_in[0].astype(BF16)
    cos2, sin2 = _rope_tables(seq)
    tn = _tile(512, kv_width)
    tm_in = _tile(512, seq)
    x2 = x.reshape(n_tok, d)
    common = dict(tn=tn, attn_width=attn_width, kv_width=kv_width, gate_start_col=gate_start)
    per_batch_in = seq // tm_in
    n_sections = 2
    assert in_width % (n_sections * tn) == 0 and attn_width % (2 * kv_width) == 0
    p_all = _inproj(x2, mod3, g_pre_mix, w_in_b, cos2, sin2, q_norm, k_norm, tm=tm_in,
                    sec_width=in_width // n_sections, sec_off=0, n_sections=n_sections,
                    mod_row=lambda i: i // per_batch_in, pos_blocks=per_batch_in, use_rope=True, **common)
    tm_ctx = _tile(512, n_ctx)
    p_ctx = _inproj(ctx.reshape(batch * n_ctx, d), mod3, g_pre_mix, w_in_b, cos2[:tm_ctx], sin2[:tm_ctx],
                    q_norm, k_norm, tm=tm_ctx, sec_width=2 * kv_width, sec_off=attn_width // (2 * kv_width),
                    n_sections=1, mod_row=lambda i: batch, pos_blocks=1, use_rope=False, **common)

    attn = _attention(p_all, p_ctx, batch=batch, seq=seq, n_ctx=n_ctx, n_kv=n_kv, group=group,
                      tq=_tile(128, seq))

    merged = _merge(attn, p_all, conv_w[0], w_attn_out[0].astype(BF16), w_conv_out[0].astype(BF16),
                    tm=_tile(256, seq), seq=seq, conv_width=conv_width, d=d,
                    conv_start_col=conv_start, gate_start_col=gate_start)

    pad = LANES - n_groups - n_experts
    w_r = jnp.concatenate([w_router_group[0], w_router_expert[0].reshape(d, n_experts), jnp.zeros((d, pad), F32)], axis=1)
    b_r = jnp.concatenate([b_router_group[0], b_router_expert[0].reshape(n_experts), jnp.zeros((pad,), F32)])[None, :]
    tm_tok = _tile(256, seq)
    h, f, route, counts = _mix_route(merged, w_mix_out[0].astype(BF16), x2, mod3, g_post_mix, g_pre_ffn, w_r, b_r,
                                     tm=tm_tok, seq=seq, n_groups=n_groups, per_group=per_group)

    bm = _tile(256, n_tok)
    n_blocks = n_tok * TOP_K // bm + n_experts
    cnt = counts[0, :n_experts].astype(jnp.int32)
    padded = (cnt + bm - 1) // bm * bm
    pend = jnp.cumsum(padded)
    pstart = pend - padded
    e1, e2 = route[:, 0].astype(jnp.int32), route[:, 1].astype(jnp.int32)
    expert_ids = jnp.arange(n_experts, dtype=jnp.int32)[None, :]

    def segment_start(e):
        return jnp.sum(jnp.where(e[:, None] == expert_ids, pstart[None, :], 0), axis=1)

    dest1 = segment_start(e1) + route[:, 4].astype(jnp.int32)
    dest2 = segment_start(e2) + route[:, 5].astype(jnp.int32)
    n_used = (pend[-1] // bm).astype(jnp.int32)
    blk_row = jnp.minimum(jnp.arange(n_blocks, dtype=jnp.int32), n_used - 1) * bm
    blk_exp = jnp.sum((pend[None, :] <= blk_row[:, None]).astype(jnp.int32), axis=1)
    blk_exp = jnp.minimum(blk_exp, n_experts - 1)
    def per_block(table):
        return jnp.sum(jnp.where(blk_exp[:, None] == expert_ids, table[None, :], 0), axis=1)

    blk_valid = jnp.clip(per_block(pstart + cnt) - blk_row, 0, bm)
    blk_first = (blk_row == per_block(pstart)).astype(jnp.int32)
    used = cnt > 0
    later = jnp.where(used[None, :] & (expert_ids > expert_ids.T), expert_ids, n_experts)
    next_used = jnp.min(later, axis=1)
    blk_next = per_block(jnp.where(next_used < n_experts, next_used, -1))
    blk_slot = per_block((jnp.cumsum(used.astype(jnp.int32)) - 1) % 2)

    xs = _dispatch(dest1, dest2, f, n_rows=n_blocks * bm, tm=tm_tok)
    yb = _experts(blk_exp, n_used.reshape(1), blk_valid, blk_first, blk_next, blk_slot, xs,
                  w_gate[0], w_up[0], w_down[0], bm=bm)
    out = _combine(dest1, dest2, yb, route, h, mod3, g_post_ffn, tm=tm_tok, seq=seq)
    return out.reshape(batch, seq, d)
```

```python
import functools

import jax
import jax.numpy as jnp
from jax import lax
from jax.experimental import pallas as pl
from jax.experimental.pallas import tpu as pltpu

HEAD_DIM = 128
GRID_W = 64
ROPE_THETA = 10000.0
EPS = 1e-6
TOP_K = 2
N_MOD = 6
LANES = 128
BF16_SUBLANES = 16
VMEM_LIMIT_BYTES = 56 * 1024 * 1024
NEG_BIG = -1e30
LOG2_E = 1.4426950408889634

F32 = jnp.float32
BF16 = jnp.bfloat16


def _params(n_grid_axes, flags=None):
    return pltpu.CompilerParams(dimension_semantics=("arbitrary",) * n_grid_axes,
                                vmem_limit_bytes=VMEM_LIMIT_BYTES, flags=flags)


def _rms(t, gain):
    ms = jnp.mean(t * t, axis=-1, keepdims=True)
    return t * lax.rsqrt(ms + EPS) * gain


def _adaln_body(c_ref, w_ref, b_ref, o_ref):
    c = c_ref[...]
    s = (c * jax.nn.sigmoid(c)).astype(BF16)
    o_ref[...] = jnp.dot(s, w_ref[...].astype(BF16), preferred_element_type=F32) + b_ref[...]


def _adaln(cc, w, b):
    rows, d = cc.shape
    n = w.shape[1]
    tn = min(1024, n)
    return pl.pallas_call(
        _adaln_body,
        grid=(n // tn,),
        in_specs=[pl.BlockSpec((rows, d), lambda j: (0, 0)),
                  pl.BlockSpec((d, tn), lambda j: (0, j)),
                  pl.BlockSpec((1, tn), lambda j: (0, j))],
        out_specs=pl.BlockSpec((rows, tn), lambda j: (0, j)),
        out_shape=jax.ShapeDtypeStruct((rows, n), F32),
        compiler_params=_params(1),
        name="adaln",
    )(cc, w, b.reshape(1, n))


Q_COLS, K_COLS, PLAIN_COLS, GATE_COLS = "q", "k", "plain", "gate"


def _inproj_body(x_ref, mod_ref, g_ref, w_ref, cos_ref, sin_ref, qn_ref, kn_ref, o_ref, *,
                 sections, tn, use_rope, q_scale):
    sec = pl.program_id(0)
    tm = x_ref.shape[0]
    half = tm // 2

    def normalised(r0):
        xn = _rms(x_ref[r0:r0 + half, :], g_ref[...])
        return (xn * (1.0 + mod_ref[0, 1:2, :]) + mod_ref[0, 0:1, :]).astype(BF16)

    def normed_heads(acc, r0, col0, gain, out_scale):
        for hh in range(tn // HEAD_DIM):
            t = _rms(acc[:, hh * HEAD_DIM:(hh + 1) * HEAD_DIM], gain)
            if use_rope:
                t = t * cos_ref[r0:r0 + half, :] + pltpu.roll(t, HEAD_DIM // 2, 1) * sin_ref[r0:r0 + half, :]
            if out_scale != 1.0:
                t = t * out_scale
            o_ref[r0:r0 + half, col0 + hh * HEAD_DIM:col0 + (hh + 1) * HEAD_DIM] = t.astype(o_ref.dtype)

    def column_tile(a, r0, col0, kind):
        acc = jnp.dot(a, w_ref[:, col0:col0 + tn], preferred_element_type=F32)
        if kind == Q_COLS:
            normed_heads(acc, r0, col0, qn_ref[...], q_scale)
        elif kind == K_COLS:
            normed_heads(acc, r0, col0, kn_ref[...], 1.0)
        elif kind == PLAIN_COLS:
            o_ref[r0:r0 + half, col0:col0 + tn] = acc.astype(o_ref.dtype)
        else:
            o_ref[r0:r0 + half, col0:col0 + tn] = jax.nn.sigmoid(acc).astype(o_ref.dtype)

    def section(kinds):
        a_lo = normalised(0)
        column_tile(a_lo, 0, 0, kinds[0])
        a_hi = normalised(half)
        column_tile(a_hi, half, 0, kinds[0])
        for t, kind in enumerate(kinds[1:], start=1):
            column_tile(a_lo, 0, t * tn, kind)
            column_tile(a_hi, half, t * tn, kind)

    for s_idx, kinds in enumerate(sections):
        pl.when(sec == s_idx)(functools.partial(section, kinds))


def _inproj(x2, mod3, g, w_b, cos2, sin2, qn, kn, *, tm, tn, sec_width, sec_off, n_sections, mod_row, pos_blocks,
            attn_width, kv_width, gate_start_col, use_rope):
    rows, d = x2.shape

    def kind_of(col):
        if col < attn_width:
            return Q_COLS
        if col < attn_width + kv_width:
            return K_COLS
        return PLAIN_COLS if col < gate_start_col else GATE_COLS

    sections = tuple(tuple(kind_of((sec_off + s) * sec_width + t * tn) for t in range(sec_width // tn))
                     for s in range(n_sections))
    body = functools.partial(_inproj_body, sections=sections, tn=tn, use_rope=use_rope,
                             q_scale=HEAD_DIM ** -0.5 * LOG2_E)
    return pl.pallas_call(
        body,
        grid=(n_sections, rows // tm),
        in_specs=[pl.BlockSpec((tm, d), lambda s, i: (i, 0)),
                  pl.BlockSpec((1, N_MOD, d), lambda s, i: (mod_row(i), 0, 0)),
                  pl.BlockSpec((1, d), lambda s, i: (0, 0)),
                  pl.BlockSpec((d, sec_width), lambda s, i: (0, s + sec_off), pipeline_mode=pl.Buffered(1)),
                  pl.BlockSpec((tm, HEAD_DIM), lambda s, i: (i % pos_blocks, 0)),
                  pl.BlockSpec((tm, HEAD_DIM), lambda s, i: (i % pos_blocks, 0)),
                  pl.BlockSpec((1, HEAD_DIM), lambda s, i: (0, 0)),
                  pl.BlockSpec((1, HEAD_DIM), lambda s, i: (0, 0))],
        out_specs=pl.BlockSpec((tm, sec_width), lambda s, i: (i, s)),
        out_shape=jax.ShapeDtypeStruct((rows, n_sections * sec_width), BF16),
        compiler_params=_params(2),
        name="inproj_rope" if use_rope else "inproj_ctx",
    )(x2, mod3, g, w_b, cos2, sin2, qn, kn)


ATTN_KEY_CHUNK = 256


def _attn_body(q_ref, k_ref, v_ref, kc_ref, vc_ref, o_ref, s0_scr, m0_scr, s1_scr, m1_scr, vt_scr, *, group, tq):
    seq = q_ref.shape[0]
    q_tiles = seq // tq
    slabs = ((s0_scr, m0_scr), (s1_scr, m1_scr))
    vt_scr[:, :seq] = v_ref[...].astype(F32).T.astype(BF16)
    vt_scr[:, seq:] = vc_ref[...].astype(F32).T.astype(BF16)

    def rows(t):
        start = t * tq
        return pl.ds(start if isinstance(start, int) else pl.multiple_of(start, tq), tq)

    n_ctx = kc_ref.shape[0]
    chunk = min(ATTN_KEY_CHUNK, seq)
    key_chunks = [(k_ref, c0, c0, chunk) for c0 in range(0, seq, chunk)] + [(kc_ref, 0, seq, n_ctx)]

    def step(t, score_slab, finish_slab):
        if score_slab is not None:
            q = jnp.concatenate([q_ref[rows(t), g * HEAD_DIM:(g + 1) * HEAD_DIM] for g in range(group)], axis=0)
            m_run = None
        if finish_slab is not None:
            m_prev = finish_slab[1][...]
            l = 0.0
            acc = 0.0
        for keys, k0, s0, size in key_chunks:
            if score_slab is not None:
                s = lax.dot_general(keys[k0:k0 + size, :], q, (((1,), (1,)), ((), ())),
                                    preferred_element_type=F32)
                score_slab[0][s0:s0 + size] = s
                m_chunk = jnp.max(s, axis=0, keepdims=True)
                m_run = m_chunk if m_run is None else jnp.maximum(m_run, m_chunk)
            if finish_slab is not None:
                p = jnp.exp2(finish_slab[0][s0:s0 + size] - m_prev)
                l = l + jnp.sum(p, axis=0, keepdims=True)
                acc = acc + jnp.dot(vt_scr[:, s0:s0 + size], p.astype(BF16), preferred_element_type=F32)
        if score_slab is not None:
            score_slab[1][...] = m_run
        if finish_slab is not None:
            out = acc * (1.0 / l)
            for g in range(group):
                o_ref[rows(t - 1), g * HEAD_DIM:(g + 1) * HEAD_DIM] = (
                    out[:, g * tq:(g + 1) * tq].T.astype(o_ref.dtype))

    step(0, slabs[0], None)
    n_pairs = (q_tiles - 1) // 2

    def pair(j, carry):
        t = 1 + 2 * j
        step(t, slabs[1], slabs[0])
        step(t + 1, slabs[0], slabs[1])
        return carry

    if n_pairs > 0:
        lax.fori_loop(0, n_pairs, pair, 0)
    last = q_tiles - 1
    if last % 2:
        step(last, slabs[1], slabs[0])
    step(q_tiles, None, slabs[last % 2])


def _attention(p_all, p_ctx, *, batch, seq, n_ctx, n_kv, group, tq):
    gw = group * HEAD_DIM
    body = functools.partial(_attn_body, group=group, tq=tq)
    n_keys = seq + n_ctx
    k_col = n_kv * group
    v_col = k_col + n_kv
    return pl.pallas_call(
        body,
        grid=(batch, n_kv),
        in_specs=[pl.BlockSpec((seq, gw), lambda b, h: (b, h)),
                  pl.BlockSpec((seq, HEAD_DIM), lambda b, h: (b, k_col + h)),
                  pl.BlockSpec((seq, HEAD_DIM), lambda b, h: (b, v_col + h)),
                  pl.BlockSpec((n_ctx, HEAD_DIM), lambda b, h: (b, h)),
                  pl.BlockSpec((n_ctx, HEAD_DIM), lambda b, h: (b, n_kv + h))],
        out_specs=pl.BlockSpec((seq, gw), lambda b, h: (b, h)),
        out_shape=jax.ShapeDtypeStruct((batch * seq, n_kv * gw), BF16),
        scratch_shapes=[pltpu.VMEM((n_keys, group * tq), F32), pltpu.VMEM((1, group * tq), F32)] * 2
        + [pltpu.VMEM((HEAD_DIM, n_keys), BF16)],
        compiler_params=_params(2),
        name="attention",
    )(p_all, p_all, p_all, p_ctx, p_ctx)


MERGE_CHUNKS = 4


def _merge_body(attn_ref, pb_ref, pc_ref, px_ref, pcp_ref, pxp_ref, pcn_ref, pxn_ref, ga_ref, gc_ref,
                cw_ref, wa_ref, wc_ref, o_ref, *, tiles_per_seq):
    i = pl.program_id(0)
    tm = attn_ref.shape[0]
    attn = attn_ref[...]
    width = o_ref.shape[1] // MERGE_CHUNKS
    y_attn0 = jnp.dot(attn, wa_ref[:, :width], preferred_element_type=F32)

    u = pc_ref[...].astype(F32) * px_ref[...].astype(F32)
    last = BF16_SUBLANES - 1
    u_before = pcp_ref[last:last + 1, :].astype(F32) * pxp_ref[last:last + 1, :].astype(F32)
    u_after = pcn_ref[0:1, :].astype(F32) * pxn_ref[0:1, :].astype(F32)
    pos = i % tiles_per_seq
    u_before = jnp.where(pos == 0, 0.0, u_before)
    u_after = jnp.where(pos == tiles_per_seq - 1, 0.0, u_after)
    row = lax.broadcasted_iota(jnp.int32, u.shape, 0)
    u_prev = jnp.where(row == 0, u_before, pltpu.roll(u, 1, 0))
    u_next = jnp.where(row == tm - 1, u_after, pltpu.roll(u, tm - 1, 0))
    conv = cw_ref[0:1, :] * u_prev + cw_ref[1:2, :] * u + cw_ref[2:3, :] * u_next
    z = (pb_ref[...].astype(F32) * conv).astype(BF16)

    for c in range(MERGE_CHUNKS):
        cols = slice(c * width, (c + 1) * width)
        y_attn = y_attn0 if c == 0 else jnp.dot(attn, wa_ref[:, cols], preferred_element_type=F32)
        y_conv = jnp.dot(z, wc_ref[:, cols], preferred_element_type=F32)
        merged = ga_ref[:, cols].astype(F32) * y_attn + gc_ref[:, cols].astype(F32) * y_conv
        o_ref[:, cols] = merged.astype(o_ref.dtype)


def _merge(attn, p_all, conv_w, wa_b, wc_b, *, tm, seq, conv_width, d, conv_start_col, gate_start_col):
    rows, attn_width = attn.shape
    hb = BF16_SUBLANES
    cb = conv_start_col // conv_width
    gb = gate_start_col // d
    n_hblocks = rows // hb
    per = tm // hb

    def prev_map(col):
        return lambda i: (jnp.maximum(i * per - 1, 0), col)

    def next_map(col):
        return lambda i: (jnp.minimum((i + 1) * per, n_hblocks - 1), col)

    body = functools.partial(_merge_body, tiles_per_seq=seq // tm)
    resident = dict(pipeline_mode=pl.Buffered(1))
    return pl.pallas_call(
        body,
        grid=(rows // tm,),
        in_specs=[pl.BlockSpec((tm, attn_width), lambda i: (i, 0)),
                  pl.BlockSpec((tm, conv_width), lambda i: (i, cb)),
                  pl.BlockSpec((tm, conv_width), lambda i: (i, cb + 1)),
                  pl.BlockSpec((tm, conv_width), lambda i: (i, cb + 2)),
                  pl.BlockSpec((hb, conv_width), prev_map(cb + 1)),
                  pl.BlockSpec((hb, conv_width), prev_map(cb + 2)),
                  pl.BlockSpec((hb, conv_width), next_map(cb + 1)),
                  pl.BlockSpec((hb, conv_width), next_map(cb + 2)),
                  pl.BlockSpec((tm, d), lambda i: (i, gb)),
                  pl.BlockSpec((tm, d), lambda i: (i, gb + 1)),
                  pl.BlockSpec(conv_w.shape, lambda i: (0, 0)),
                  pl.BlockSpec(wa_b.shape, lambda i: (0, 0), **resident),
                  pl.BlockSpec(wc_b.shape, lambda i: (0, 0), **resident)],
        out_specs=pl.BlockSpec((tm, d), lambda i: (i, 0)),
        out_shape=jax.ShapeDtypeStruct((rows, d), BF16),
        compiler_params=_params(1),
        name="merge",
    )(attn, p_all, p_all, p_all, p_all, p_all, p_all, p_all, p_all, p_all, conv_w, wa_b, wc_b)


def _split_bf16(t):
    hi = t.astype(BF16)
    lo = (t - hi.astype(F32)).astype(BF16)
    return hi, lo


def _mix_route_body(m_ref, wm_ref, x_ref, mod_ref, gpost_ref, gpre_ref, wr_ref, br_ref,
                    h_ref, f_ref, route_ref, cnt_ref, carry_scr, f0_scr, f1_scr, *, n_groups, per_group, n_tiles):
    i = pl.program_id(0)
    f_slots = (f0_scr, f1_scr)

    def project(f_scr, stages=()):
        m = m_ref[...]
        stages = iter(stages)
        width = wm_ref.shape[1] // ROUTE_STAGES
        ys = []
        for c in range(ROUTE_STAGES):
            ys.append(jnp.dot(m, wm_ref[:, c * width:(c + 1) * width], preferred_element_type=F32))
            next(stages, None)
        y = jnp.concatenate(ys, axis=1)
        h = x_ref[...] + mod_ref[0, 2:3, :] * _rms(y, gpost_ref[...])
        h_ref[...] = h
        f = _rms(h, gpre_ref[...]) * (1.0 + mod_ref[0, 4:5, :]) + mod_ref[0, 3:4, :]
        f_ref[...] = f
        f_scr[...] = f

    route = functools.partial(_route_stages, wr_ref, br_ref, route_ref, cnt_ref, carry_scr,
                              n_groups=n_groups, per_group=per_group)

    @pl.when(i == 0)
    def _():
        carry_scr[...] = jnp.zeros_like(carry_scr)
        project(f_slots[0])

    for parity in range(2):
        @pl.when((i % 2 == parity) & (i > 0) & (i < n_tiles))
        def _(parity=parity):
            project(f_slots[parity], route(f_slots[1 - parity]))

    @pl.when(i == n_tiles)
    def _():
        for _ in route(f_slots[(n_tiles - 1) % 2]):
            pass


ROUTE_STAGES = 4


def _route_stages(wr_ref, br_ref, route_ref, cnt_ref, carry_scr, f_scr, *, n_groups, per_group):
    f = f_scr[...]
    tm = f.shape[0]
    f_hi, f_lo = _split_bf16(f)
    w_hi, w_lo = _split_bf16(wr_ref[...])
    logits = (jnp.dot(f_hi, w_hi, preferred_element_type=F32)
              + jnp.dot(f_hi, w_lo, preferred_element_type=F32)
              + jnp.dot(f_lo, w_hi, preferred_element_type=F32)) + br_ref[...]
    yield

    lane = lax.broadcasted_iota(jnp.int32, logits.shape, 1)

    def first_argmax(vals):
        mx = jnp.max(vals, axis=-1, keepdims=True)
        idx = jnp.min(jnp.where(vals == mx, lane, LANES), axis=-1, keepdims=True)
        return mx, idx

    is_group = lane < n_groups
    g_max, g_idx = first_argmax(jnp.where(is_group, logits, NEG_BIG))
    g_den = jnp.sum(jnp.where(is_group, jnp.exp(logits - g_max), 0.0), axis=-1, keepdims=True)
    grp_p = 1.0 / g_den
    yield
    lo_lane = n_groups + g_idx * per_group
    e_logits = jnp.where((lane >= lo_lane) & (lane < lo_lane + per_group), logits, NEG_BIG)
    l1, i1 = first_argmax(e_logits)
    l2, i2 = first_argmax(jnp.where(lane == i1, NEG_BIG, e_logits))
    r = jnp.exp(l2 - l1)
    gate1 = grp_p / (1.0 + r)
    gate2 = grp_p * r / (1.0 + r)
    e1 = i1 - n_groups
    e2 = i2 - n_groups
    yield

    onehot = ((lane == e1) | (lane == e2)).astype(BF16)
    r_i = lax.broadcasted_iota(jnp.int32, (tm, tm), 0)
    c_i = lax.broadcasted_iota(jnp.int32, (tm, tm), 1)
    before = (c_i < r_i).astype(BF16)
    seen = jnp.dot(before, onehot, preferred_element_type=F32) + carry_scr[0:1, :]
    rank1 = jnp.sum(jnp.where(lane == e1, seen, 0.0), axis=-1, keepdims=True)
    rank2 = jnp.sum(jnp.where(lane == e2, seen, 0.0), axis=-1, keepdims=True)
    carry_scr[...] = carry_scr[...] + jnp.sum(onehot.astype(F32), axis=0, keepdims=True)
    cnt_ref[...] = carry_scr[...]

    route = jnp.where(lane == 0, e1.astype(F32), 0.0)
    route = jnp.where(lane == 1, e2.astype(F32), route)
    route = jnp.where(lane == 2, gate1, route)
    route = jnp.where(lane == 3, gate2, route)
    route = jnp.where(lane == 4, rank1, route)
    route = jnp.where(lane == 5, rank2, route)
    route_ref[...] = route
    yield


def _mix_route(merged, wm_b, x2, mod3, gpost, gpre, w_r, b_r, *, tm, seq, n_groups, per_group):
    rows, d = x2.shape
    per_batch = seq // tm
    n_tiles = rows // tm
    body = functools.partial(_mix_route_body, n_groups=n_groups, per_group=per_group, n_tiles=n_tiles)

    def tile(i):
        return jnp.minimum(i, n_tiles - 1)

    def routed(i):
        return jnp.maximum(i - 1, 0)

    return pl.pallas_call(
        body,
        grid=(n_tiles + 1,),
        in_specs=[pl.BlockSpec((tm, d), lambda i: (tile(i), 0)),
                  pl.BlockSpec(wm_b.shape, lambda i: (0, 0), pipeline_mode=pl.Buffered(1)),
                  pl.BlockSpec((tm, d), lambda i: (tile(i), 0)),
                  pl.BlockSpec((1, N_MOD, d), lambda i: (tile(i) // per_batch, 0, 0)),
                  pl.BlockSpec((1, d), lambda i: (0, 0)),
                  pl.BlockSpec((1, d), lambda i: (0, 0)),
                  pl.BlockSpec((d, LANES), lambda i: (0, 0)),
                  pl.BlockSpec((1, LANES), lambda i: (0, 0))],
        out_specs=[pl.BlockSpec((tm, d), lambda i: (tile(i), 0)),
                   pl.BlockSpec((tm, d), lambda i: (tile(i), 0)),
                   pl.BlockSpec((tm, LANES), lambda i: (routed(i), 0)),
                   pl.BlockSpec((8, LANES), lambda i: (0, 0))],
        out_shape=[jax.ShapeDtypeStruct((rows, d), F32),
                   jax.ShapeDtypeStruct((rows, d), F32),
                   jax.ShapeDtypeStruct((rows, LANES), F32),
                   jax.ShapeDtypeStruct((8, LANES), F32)],
        scratch_shapes=[pltpu.VMEM((8, LANES), F32), pltpu.VMEM((tm, d), F32), pltpu.VMEM((tm, d), F32)],
        compiler_params=_params(1),
        name="mix_route",
    )(merged, wm_b, x2, mod3, gpost, gpre, w_r, b_r)


DMA_ISSUE_UNROLL = 8


def _dispatch_body(d1_ref, d2_ref, f_ref, xs_hbm, stage, sem, *, n_steps):
    i = pl.program_id(0)
    tm = f_ref.shape[0]
    slot = i % 2

    def drain(s):
        for _ in range(TOP_K):
            pltpu.make_async_copy(stage.at[s], xs_hbm.at[pl.ds(0, tm)], sem.at[s]).wait()

    @pl.when(i >= 2)
    def _():
        drain(slot)

    stage[slot] = f_ref[...].reshape(stage.shape[1:])
    base = i * tm

    def body(r, carry):
        t = base + r
        src = stage.at[slot, r]
        pltpu.make_async_copy(src, xs_hbm.at[d1_ref[t]], sem.at[slot]).start()
        pltpu.make_async_copy(src, xs_hbm.at[d2_ref[t]], sem.at[slot]).start(priority=1)
        return carry

    lax.fori_loop(0, tm, body, 0, unroll=DMA_ISSUE_UNROLL)

    @pl.when(i == n_steps - 1)
    def _():
        drain(slot)
        if n_steps >= 2:
            drain(1 - slot)


def _dispatch(dest1, dest2, f, *, n_rows, tm):
    n_tok, d = f.shape
    n_steps = n_tok // tm
    body = functools.partial(_dispatch_body, n_steps=n_steps)
    grid_spec = pltpu.PrefetchScalarGridSpec(
        num_scalar_prefetch=2,
        grid=(n_steps,),
        in_specs=[pl.BlockSpec((tm, d), lambda i, d1, d2: (i, 0))],
        out_specs=pl.BlockSpec(memory_space=pl.ANY),
        scratch_shapes=[pltpu.VMEM((2, tm, d // LANES, LANES), f.dtype),
                        pltpu.SemaphoreType.DMA((2,))],
    )
    return pl.pallas_call(
        body,
        grid_spec=grid_spec,
        out_shape=jax.ShapeDtypeStruct((n_rows, d // LANES, LANES), f.dtype),
        compiler_params=_params(1),
        name="dispatch",
    )(dest1, dest2, f)


def _experts_body(be_ref, nu_ref, valid_ref, first_ref, next_ref, slot_ref,
                  x_ref, wg_hbm, wu_hbm, wd_hbm, o_ref, wg_f, wu_f, wd_f, wsem, wg_b, wu_b, wd_b):
    i = pl.program_id(0)
    weights = ((wg_hbm, wg_f, wg_b), (wu_hbm, wu_f, wu_b), (wd_hbm, wd_f, wd_b))

    def weight_copies(expert, slot):
        return [pltpu.make_async_copy(hbm.at[expert], ring.at[slot], wsem.at[slot, k])
                for k, (hbm, ring, _) in enumerate(weights)]

    @pl.when(i == 0)
    def _():
        for cp in weight_copies(be_ref[0], 0):
            cp.start()

    @pl.when(i < nu_ref[0])
    def _():
        @pl.when(first_ref[i] == 1)
        def _():
            slot = slot_ref[i]
            for cp in weight_copies(be_ref[i], slot):
                cp.wait()
            for _, ring, cast in weights:
                cast[...] = ring[slot].astype(BF16)

            @pl.when(next_ref[i] >= 0)
            def _():
                for cp in weight_copies(next_ref[i], 1 - slot):
                    cp.start()

        bm = x_ref.shape[0]
        xr = x_ref[...].reshape(bm, wg_b.shape[0])
        row = lax.broadcasted_iota(jnp.int32, xr.shape, 0)
        x = jnp.where(row < valid_ref[i], xr, 0.0).astype(BF16)
        g = jnp.dot(x, wg_b[...], preferred_element_type=F32)
        u = jnp.dot(x, wu_b[...], preferred_element_type=F32)
        hid = (g * jax.nn.sigmoid(g) * u).astype(BF16)
        o_ref[...] = jnp.dot(hid, wd_b[...], preferred_element_type=F32).reshape(o_ref.shape)


def _experts(block_expert, n_used, block_valid, block_first, block_next, block_slot, xs, w_gate, w_up, w_down, *, bm):
    n_blocks = block_expert.shape[0]
    _, d, de = w_gate.shape

    def row_block(i, be, nu, *_):
        return (jnp.minimum(i, nu[0] - 1), 0, 0)

    grid_spec = pltpu.PrefetchScalarGridSpec(
        num_scalar_prefetch=6,
        grid=(n_blocks,),
        in_specs=[pl.BlockSpec((bm, d // LANES, LANES), row_block),
                  pl.BlockSpec(memory_space=pl.ANY),
                  pl.BlockSpec(memory_space=pl.ANY),
                  pl.BlockSpec(memory_space=pl.ANY)],
        out_specs=pl.BlockSpec((bm, d // LANES, LANES), row_block),
        scratch_shapes=[pltpu.VMEM((2, d, de), F32),
                        pltpu.VMEM((2, d, de), F32),
                        pltpu.VMEM((2, de, d), F32),
                        pltpu.SemaphoreType.DMA((2, 3)),
                        pltpu.VMEM((d, de), BF16),
                        pltpu.VMEM((d, de), BF16),
                        pltpu.VMEM((de, d), BF16)],
    )
    return pl.pallas_call(
        _experts_body,
        grid_spec=grid_spec,
        out_shape=jax.ShapeDtypeStruct((n_blocks * bm, d // LANES, LANES), F32),
        compiler_params=_params(1),
        name="experts",
    )(block_expert, n_used, block_valid, block_first, block_next, block_slot, xs, w_gate, w_up, w_down)


def _combine_body(d1_ref, d2_ref, yb_hbm, route_ref, h_ref, mod_ref, g_ref, o_ref, ybuf, sem):
    i = pl.program_id(0)
    n = pl.num_programs(0)
    tm = h_ref.shape[0]

    def start_gather(blk, slot):
        base = blk * tm

        def body(r, carry):
            pltpu.make_async_copy(yb_hbm.at[d1_ref[base + r]], ybuf.at[slot, 0, r], sem.at[slot]).start()
            pltpu.make_async_copy(yb_hbm.at[d2_ref[base + r]], ybuf.at[slot, 1, r], sem.at[slot]).start(priority=1)
            return carry

        lax.fori_loop(0, tm, body, 0, unroll=DMA_ISSUE_UNROLL)

    def wait_gather(slot):
        for k in range(TOP_K):
            pltpu.make_async_copy(yb_hbm.at[pl.ds(0, tm)], ybuf.at[slot, k], sem.at[slot]).wait()

    @pl.when(i == 0)
    def _():
        start_gather(0, 0)

    @pl.when(i + 1 < n)
    def _():
        start_gather(i + 1, (i + 1) % 2)

    slot = i % 2
    wait_gather(slot)
    y = (route_ref[:, 2:3] * ybuf[slot, 0].reshape(h_ref.shape)
         + route_ref[:, 3:4] * ybuf[slot, 1].reshape(h_ref.shape))
    o_ref[...] = h_ref[...] + mod_ref[0, 5:6, :] * _rms(y, g_ref[...])


def _combine(dest1, dest2, yb, route, h, mod3, gpost, *, tm, seq):
    rows, d = h.shape
    per_batch = seq // tm
    grid_spec = pltpu.PrefetchScalarGridSpec(
        num_scalar_prefetch=2,
        grid=(rows // tm,),
        in_specs=[pl.BlockSpec(memory_space=pl.ANY),
                  pl.BlockSpec((tm, LANES), lambda i, d1, d2: (i, 0)),
                  pl.BlockSpec((tm, d), lambda i, d1, d2: (i, 0)),
                  pl.BlockSpec((1, N_MOD, d), lambda i, d1, d2: (i // per_batch, 0, 0)),
                  pl.BlockSpec((1, d), lambda i, d1, d2: (0, 0))],
        out_specs=pl.BlockSpec((tm, d), lambda i, d1, d2: (i, 0)),
        scratch_shapes=[pltpu.VMEM((2, TOP_K, tm, d // LANES, LANES), F32),
                        pltpu.SemaphoreType.DMA((2,))],
    )
    return pl.pallas_call(
        _combine_body,
        grid_spec=grid_spec,
        out_shape=jax.ShapeDtypeStruct((rows, d), F32),
        compiler_params=_params(1),
        name="combine",
    )(dest1, dest2, yb, route, h, mod3, gpost)


def _rope_tables(seq):
    rows = seq // GRID_W
    row = jnp.repeat(jnp.arange(rows, dtype=F32), GRID_W)
    col = jnp.tile(jnp.arange(GRID_W, dtype=F32), rows)
    pairs = HEAD_DIM // 4
    freqs = ROPE_THETA ** (-jnp.arange(pairs, dtype=F32) / pairs)
    ang = jnp.concatenate([row[:, None] * freqs, col[:, None] * freqs], axis=-1)
    cos, sin = jnp.cos(ang), jnp.sin(ang)
    return jnp.concatenate([cos, cos], axis=-1), jnp.concatenate([-sin, sin], axis=-1)


def _tile(limit, size):
    t = min(limit, size)
    assert size % t == 0, (limit, size)
    return t


def kernel(x, c, ctx, c_ctx, w_ada, b_ada, g_pre_mix, w_in, q_norm, k_norm, conv_w, w_attn_out, w_conv_out,
           w_mix_out, g_post_mix, g_pre_ffn, w_router_group, b_router_group, w_router_expert, b_router_expert,
           w_gate, w_up, w_down, g_post_ffn):
    assert w_ada.shape[0] == 1, "single-layer problem"
    batch, seq, d = x.shape
    n_ctx = ctx.shape[1]
    attn_width = w_attn_out.shape[1]
    conv_width = conv_w.shape[-1]
    in_width = w_in.shape[-1]
    kv_width = (in_width - attn_width - 3 * conv_width - 2 * d) // 2
    n_kv = kv_width // HEAD_DIM
    group = attn_width // kv_width
    conv_start = attn_width + 2 * kv_width
    gate_start = conv_start + 3 * conv_width
    n_groups, per_group = w_router_expert.shape[2], w_router_expert.shape[3]
    n_experts = n_groups * per_group
    assert n_groups + n_experts <= LANES and batch + 1 <= 8
    n_tok = batch * seq

    cc = jnp.concatenate([c, c_ctx[None, :], jnp.zeros((8 - batch - 1, d), F32)], axis=0)
    mod3 = _adaln(cc, w_ada[0], b_ada[0]).reshape(8, N_MOD, d)

    w_in_b = w_in[0].astype(BF16)
    cos2, sin2 = _rope_tables(seq)
    tn = _tile(512, kv_width)
    tm_in = _tile(512, seq)
    x2 = x.reshape(n_tok, d)
    common = dict(tn=tn, attn_width=attn_width, kv_width=kv_width, gate_start_col=gate_start)
    per_batch_in = seq // tm_in
    n_sections = 2
    assert in_width % (n_sections * tn) == 0 and attn_width % (2 * kv_width) == 0
    p_all = _inproj(x2, mod3, g_pre_mix, w_in_b, cos2, sin2, q_norm, k_norm, tm=tm_in,
                    sec_width=in_width // n_sections, sec_off=0, n_sections=n_sections,
                    mod_row=lambda i: i // per_batch_in, pos_blocks=per_batch_in, use_rope=True, **common)
    tm_ctx = _tile(512, n_ctx)
    p_ctx = _inproj(ctx.reshape(batch * n_ctx, d), mod3, g_pre_mix, w_in_b, cos2[:tm_ctx], sin2[:tm_ctx],
                    q_norm, k_norm, tm=tm_ctx, sec_width=2 * kv_width, sec_off=attn_width // (2 * kv_width),
                    n_sections=1, mod_row=lambda i: batch, pos_blocks=1, use_rope=False, **common)

    attn = _attention(p_all, p_ctx, batch=batch, seq=seq, n_ctx=n_ctx, n_kv=n_kv, group=group,
                      tq=_tile(128, seq))

    merged = _merge(attn, p_all, conv_w[0], w_attn_out[0].astype(BF16), w_conv_out[0].astype(BF16),
                    tm=_tile(256, seq), seq=seq, conv_width=conv_width, d=d,
                    conv_start_col=conv_start, gate_start_col=gate_start)

    pad = LANES - n_groups - n_experts
    w_r = jnp.concatenate([w_router_group[0], w_router_expert[0].reshape(d, n_experts), jnp.zeros((d, pad), F32)], axis=1)
    b_r = jnp.concatenate([b_router_group[0], b_router_expert[0].reshape(n_experts), jnp.zeros((pad,), F32)])[None, :]
    tm_tok = _tile(256, seq)
    h, f, route, counts = _mix_route(merged, w_mix_out[0].astype(BF16), x2, mod3, g_post_mix, g_pre_ffn, w_r, b_r,
                                     tm=tm_tok, seq=seq, n_groups=n_groups, per_group=per_group)

    bm = _tile(256, n_tok)
    n_blocks = n_tok * TOP_K // bm + n_experts
    cnt = counts[0, :n_experts].astype(jnp.int32)
    padded = (cnt + bm - 1) // bm * bm
    pend = jnp.cumsum(padded)
    pstart = pend - padded
    e1, e2 = route[:, 0].astype(jnp.int32), route[:, 1].astype(jnp.int32)
    expert_ids = jnp.arange(n_experts, dtype=jnp.int32)[None, :]

    def segment_start(e):
        return jnp.sum(jnp.where(e[:, None] == expert_ids, pstart[None, :], 0), axis=1)

    dest1 = segment_start(e1) + route[:, 4].astype(jnp.int32)
    dest2 = segment_start(e2) + route[:, 5].astype(jnp.int32)
    n_used = (pend[-1] // bm).astype(jnp.int32)
    blk_row = jnp.minimum(jnp.arange(n_blocks, dtype=jnp.int32), n_used - 1) * bm
    blk_exp = jnp.sum((pend[None, :] <= blk_row[:, None]).astype(jnp.int32), axis=1)
    blk_exp = jnp.minimum(blk_exp, n_experts - 1)
    def per_block(table):
        return jnp.sum(jnp.where(blk_exp[:, None] == expert_ids, table[None, :], 0), axis=1)

    blk_valid = jnp.clip(per_block(pstart + cnt) - blk_row, 0, bm)
    blk_first = (blk_row == per_block(pstart)).astype(jnp.int32)
    used = cnt > 0
    later = jnp.where(used[None, :] & (expert_ids > expert_ids.T), expert_ids, n_experts)
    next_used = jnp.min(later, axis=1)
    blk_next = per_block(jnp.where(next_used < n_experts, next_used, -1))
    blk_slot = per_block((jnp.cumsum(used.astype(jnp.int32)) - 1) % 2)

    xs = _dispatch(dest1, dest2, f, n_rows=n_blocks * bm, tm=tm_tok)
    yb = _experts(blk_exp, n_used.reshape(1), blk_valid, blk_first, blk_next, blk_slot, xs,
                  w_gate[0], w_up[0], w_down[0], bm=bm)
    out = _combine(dest1, dest2, yb, route, h, mod3, g_post_ffn, tm=tm_tok, seq=seq)
    return out.reshape(batch, seq, d)
```

```python
import functools

import jax
import jax.numpy as jnp
from jax import lax
from jax.experimental import pallas as pl
from jax.experimental.pallas import tpu as pltpu

HEAD_DIM = 128
GRID_W = 64
ROPE_THETA = 10000.0
EPS = 1e-6
TOP_K = 2
N_MOD = 6
LANES = 128
BF16_SUBLANES = 16
VMEM_LIMIT_BYTES = 56 * 1024 * 1024
NEG_BIG = -1e30
LOG2_E = 1.4426950408889634

PROJ_ROW_TILE = 512
PROJ_COL_TILE = 512
ATTN_QUERY_TILE = 128
TOKEN_TILE = 256
EXPERT_BLOCK_ROWS = 256

F32 = jnp.float32
BF16 = jnp.bfloat16


def _params(n_grid_axes):
    return pltpu.CompilerParams(dimension_semantics=("arbitrary",) * n_grid_axes,
                                vmem_limit_bytes=VMEM_LIMIT_BYTES)


def _rms(t, gain):
    ms = jnp.mean(t * t, axis=-1, keepdims=True)
    return t * lax.rsqrt(ms + EPS) * gain


def _adaln_body(c_ref, w_ref, b_ref, o_ref):
    c = c_ref[...]
    s = (c * jax.nn.sigmoid(c)).astype(BF16)
    o_ref[...] = jnp.dot(s, w_ref[...].astype(BF16), preferred_element_type=F32) + b_ref[...]


def _adaln(cc, w, b):
    rows, d = cc.shape
    n = w.shape[1]
    tn = min(1024, n)
    return pl.pallas_call(
        _adaln_body,
        grid=(n // tn,),
        in_specs=[pl.BlockSpec((rows, d), lambda j: (0, 0)),
                  pl.BlockSpec((d, tn), lambda j: (0, j)),
                  pl.BlockSpec((1, tn), lambda j: (0, j))],
        out_specs=pl.BlockSpec((rows, tn), lambda j: (0, j)),
        out_shape=jax.ShapeDtypeStruct((rows, n), F32),
        compiler_params=_params(1),
        name="adaln",
    )(cc, w, b.reshape(1, n))


Q_COLS, K_COLS, PLAIN_COLS, GATE_COLS = "q", "k", "plain", "gate"


def _inproj_body(x_ref, mod_ref, g_ref, w_ref, cos_ref, sin_ref, qn_ref, kn_ref, o_ref, *,
                 sections, tn, use_rope, q_scale):
    sec = pl.program_id(0)
    tm = x_ref.shape[0]
    half = tm // 2

    def normalised(r0):
        xn = _rms(x_ref[r0:r0 + half, :], g_ref[...])
        return (xn * (1.0 + mod_ref[0, 1:2, :]) + mod_ref[0, 0:1, :]).astype(BF16)

    def normed_heads(acc, r0, col0, gain, out_scale):
        for hh in range(tn // HEAD_DIM):
            t = _rms(acc[:, hh * HEAD_DIM:(hh + 1) * HEAD_DIM], gain)
            if use_rope:
                t = t * cos_ref[r0:r0 + half, :] + pltpu.roll(t, HEAD_DIM // 2, 1) * sin_ref[r0:r0 + half, :]
            if out_scale != 1.0:
                t = t * out_scale
            o_ref[r0:r0 + half, col0 + hh * HEAD_DIM:col0 + (hh + 1) * HEAD_DIM] = t.astype(o_ref.dtype)

    def column_tile(a, r0, col0, kind):
        acc = jnp.dot(a, w_ref[:, col0:col0 + tn], preferred_element_type=F32)
        if kind == Q_COLS:
            normed_heads(acc, r0, col0, qn_ref[...], q_scale)
        elif kind == K_COLS:
            normed_heads(acc, r0, col0, kn_ref[...], 1.0)
        elif kind == PLAIN_COLS:
            o_ref[r0:r0 + half, col0:col0 + tn] = acc.astype(o_ref.dtype)
        else:
            o_ref[r0:r0 + half, col0:col0 + tn] = jax.nn.sigmoid(acc).astype(o_ref.dtype)

    def section(kinds):
        a_lo = normalised(0)
        column_tile(a_lo, 0, 0, kinds[0])
        a_hi = normalised(half)
        column_tile(a_hi, half, 0, kinds[0])
        for t, kind in enumerate(kinds[1:], start=1):
            column_tile(a_lo, 0, t * tn, kind)
            column_tile(a_hi, half, t * tn, kind)

    for s_idx, kinds in enumerate(sections):
        pl.when(sec == s_idx)(functools.partial(section, kinds))


def _inproj(x2, mod3, g, w_b, cos2, sin2, qn, kn, *, tm, tn, sec_width, sec_off, n_sections, mod_row, pos_blocks,
            attn_width, kv_width, gate_start_col, use_rope):
    rows, d = x2.shape

    def kind_of(col):
        if col < attn_width:
            return Q_COLS
        if col < attn_width + kv_width:
            return K_COLS
        return PLAIN_COLS if col < gate_start_col else GATE_COLS

    sections = tuple(tuple(kind_of((sec_off + s) * sec_width + t * tn) for t in range(sec_width // tn))
                     for s in range(n_sections))
    body = functools.partial(_inproj_body, sections=sections, tn=tn, use_rope=use_rope,
                             q_scale=HEAD_DIM ** -0.5 * LOG2_E)
    return pl.pallas_call(
        body,
        grid=(n_sections, rows // tm),
        in_specs=[pl.BlockSpec((tm, d), lambda s, i: (i, 0)),
                  pl.BlockSpec((1, N_MOD, d), lambda s, i: (mod_row(i), 0, 0)),
                  pl.BlockSpec((1, d), lambda s, i: (0, 0)),
                  pl.BlockSpec((d, sec_width), lambda s, i: (0, s + sec_off), pipeline_mode=pl.Buffered(1)),
                  pl.BlockSpec((tm, HEAD_DIM), lambda s, i: (i % pos_blocks, 0)),
                  pl.BlockSpec((tm, HEAD_DIM), lambda s, i: (i % pos_blocks, 0)),
                  pl.BlockSpec((1, HEAD_DIM), lambda s, i: (0, 0)),
                  pl.BlockSpec((1, HEAD_DIM), lambda s, i: (0, 0))],
        out_specs=pl.BlockSpec((tm, sec_width), lambda s, i: (i, s)),
        out_shape=jax.ShapeDtypeStruct((rows, n_sections * sec_width), BF16),
        compiler_params=_params(2),
        name="inproj_rope" if use_rope else "inproj_ctx",
    )(x2, mod3, g, w_b, cos2, sin2, qn, kn)


ATTN_KEY_CHUNK = 256


def _attn_body(q_ref, k_ref, v_ref, kc_ref, vc_ref, o_ref, s0_scr, m0_scr, s1_scr, m1_scr, vt_scr, *, group, tq):
    seq = q_ref.shape[0]
    q_tiles = seq // tq
    slabs = ((s0_scr, m0_scr), (s1_scr, m1_scr))
    vt_scr[:, :seq] = v_ref[...].astype(F32).T.astype(BF16)
    vt_scr[:, seq:] = vc_ref[...].astype(F32).T.astype(BF16)

    def rows(t):
        start = t * tq
        return pl.ds(start if isinstance(start, int) else pl.multiple_of(start, tq), tq)

    n_ctx = kc_ref.shape[0]
    chunk = min(ATTN_KEY_CHUNK, seq)
    key_chunks = [(k_ref, c0, c0, chunk) for c0 in range(0, seq, chunk)] + [(kc_ref, 0, seq, n_ctx)]

    def step(t, score_slab, finish_slab):
        if score_slab is not None:
            q = jnp.concatenate([q_ref[rows(t), g * HEAD_DIM:(g + 1) * HEAD_DIM] for g in range(group)], axis=0)
            m_run = None
        if finish_slab is not None:
            m_prev = finish_slab[1][...]
            l = 0.0
            acc = 0.0
        for keys, k0, s0, size in key_chunks:
            if score_slab is not None:
                s = lax.dot_general(keys[k0:k0 + size, :], q, (((1,), (1,)), ((), ())),
                                    preferred_element_type=F32)
                score_slab[0][s0:s0 + size] = s
                m_chunk = jnp.max(s, axis=0, keepdims=True)
                m_run = m_chunk if m_run is None else jnp.maximum(m_run, m_chunk)
            if finish_slab is not None:
                p = jnp.exp2(finish_slab[0][s0:s0 + size] - m_prev)
                l = l + jnp.sum(p, axis=0, keepdims=True)
                acc = acc + jnp.dot(vt_scr[:, s0:s0 + size], p.astype(BF16), preferred_element_type=F32)
        if score_slab is not None:
            score_slab[1][...] = m_run
        if finish_slab is not None:
            out = acc * (1.0 / l)
            for g in range(group):
                o_ref[rows(t - 1), g * HEAD_DIM:(g + 1) * HEAD_DIM] = (
                    out[:, g * tq:(g + 1) * tq].T.astype(o_ref.dtype))

    step(0, slabs[0], None)
    n_pairs = (q_tiles - 1) // 2

    def pair(j, carry):
        t = 1 + 2 * j
        step(t, slabs[1], slabs[0])
        step(t + 1, slabs[0], slabs[1])
        return carry

    if n_pairs > 0:
        lax.fori_loop(0, n_pairs, pair, 0)
    last = q_tiles - 1
    if last % 2:
        step(last, slabs[1], slabs[0])
    step(q_tiles, None, slabs[last % 2])


def _attention(p_all, p_ctx, *, batch, seq, n_ctx, n_kv, group, tq):
    gw = group * HEAD_DIM
    body = functools.partial(_attn_body, group=group, tq=tq)
    n_keys = seq + n_ctx
    k_col = n_kv * group
    v_col = k_col + n_kv
    return pl.pallas_call(
        body,
        grid=(batch, n_kv),
        in_specs=[pl.BlockSpec((seq, gw), lambda b, h: (b, h)),
                  pl.BlockSpec((seq, HEAD_DIM), lambda b, h: (b, k_col + h)),
                  pl.BlockSpec((seq, HEAD_DIM), lambda b, h: (b, v_col + h)),
                  pl.BlockSpec((n_ctx, HEAD_DIM), lambda b, h: (b, h)),
                  pl.BlockSpec((n_ctx, HEAD_DIM), lambda b, h: (b, n_kv + h))],
        out_specs=pl.BlockSpec((seq, gw), lambda b, h: (b, h)),
        out_shape=jax.ShapeDtypeStruct((batch * seq, n_kv * gw), BF16),
        scratch_shapes=[pltpu.VMEM((n_keys, group * tq), F32), pltpu.VMEM((1, group * tq), F32)] * 2
        + [pltpu.VMEM((HEAD_DIM, n_keys), BF16)],
        compiler_params=_params(2),
        name="attention",
    )(p_all, p_all, p_all, p_ctx, p_ctx)


MERGE_CHUNKS = 4


def _merge_body(attn_ref, pb_ref, pc_ref, px_ref, pcp_ref, pxp_ref, pcn_ref, pxn_ref, ga_ref, gc_ref,
                cw_ref, wa_ref, wc_ref, o_ref, *, tiles_per_seq):
    i = pl.program_id(0)
    tm = attn_ref.shape[0]
    attn = attn_ref[...]
    width = o_ref.shape[1] // MERGE_CHUNKS
    y_attn0 = jnp.dot(attn, wa_ref[:, :width], preferred_element_type=F32)

    u = pc_ref[...].astype(F32) * px_ref[...].astype(F32)
    last = BF16_SUBLANES - 1
    u_before = pcp_ref[last:last + 1, :].astype(F32) * pxp_ref[last:last + 1, :].astype(F32)
    u_after = pcn_ref[0:1, :].astype(F32) * pxn_ref[0:1, :].astype(F32)
    pos = i % tiles_per_seq
    u_before = jnp.where(pos == 0, 0.0, u_before)
    u_after = jnp.where(pos == tiles_per_seq - 1, 0.0, u_after)
    row = lax.broadcasted_iota(jnp.int32, u.shape, 0)
    u_prev = jnp.where(row == 0, u_before, pltpu.roll(u, 1, 0))
    u_next = jnp.where(row == tm - 1, u_after, pltpu.roll(u, tm - 1, 0))
    conv = cw_ref[0:1, :] * u_prev + cw_ref[1:2, :] * u + cw_ref[2:3, :] * u_next
    z = (pb_ref[...].astype(F32) * conv).astype(BF16)

    for c in range(MERGE_CHUNKS):
        cols = slice(c * width, (c + 1) * width)
        y_attn = y_attn0 if c == 0 else jnp.dot(attn, wa_ref[:, cols], preferred_element_type=F32)
        y_conv = jnp.dot(z, wc_ref[:, cols], preferred_element_type=F32)
        merged = ga_ref[:, cols].astype(F32) * y_attn + gc_ref[:, cols].astype(F32) * y_conv
        o_ref[:, cols] = merged.astype(o_ref.dtype)


def _merge(attn, p_all, conv_w, wa_b, wc_b, *, tm, seq, conv_width, d, conv_start_col, gate_start_col):
    rows, attn_width = attn.shape
    hb = BF16_SUBLANES
    cb = conv_start_col // conv_width
    gb = gate_start_col // d
    n_hblocks = rows // hb
    per = tm // hb

    def prev_map(col):
        return lambda i: (jnp.maximum(i * per - 1, 0), col)

    def next_map(col):
        return lambda i: (jnp.minimum((i + 1) * per, n_hblocks - 1), col)

    body = functools.partial(_merge_body, tiles_per_seq=seq // tm)
    resident = dict(pipeline_mode=pl.Buffered(1))
    return pl.pallas_call(
        body,
        grid=(rows // tm,),
        in_specs=[pl.BlockSpec((tm, attn_width), lambda i: (i, 0)),
                  pl.BlockSpec((tm, conv_width), lambda i: (i, cb)),
                  pl.BlockSpec((tm, conv_width), lambda i: (i, cb + 1)),
                  pl.BlockSpec((tm, conv_width), lambda i: (i, cb + 2)),
                  pl.BlockSpec((hb, conv_width), prev_map(cb + 1)),
                  pl.BlockSpec((hb, conv_width), prev_map(cb + 2)),
                  pl.BlockSpec((hb, conv_width), next_map(cb + 1)),
                  pl.BlockSpec((hb, conv_width), next_map(cb + 2)),
                  pl.BlockSpec((tm, d), lambda i: (i, gb)),
                  pl.BlockSpec((tm, d), lambda i: (i, gb + 1)),
                  pl.BlockSpec(conv_w.shape, lambda i: (0, 0)),
                  pl.BlockSpec(wa_b.shape, lambda i: (0, 0), **resident),
                  pl.BlockSpec(wc_b.shape, lambda i: (0, 0), **resident)],
        out_specs=pl.BlockSpec((tm, d), lambda i: (i, 0)),
        out_shape=jax.ShapeDtypeStruct((rows, d), BF16),
        compiler_params=_params(1),
        name="merge",
    )(attn, p_all, p_all, p_all, p_all, p_all, p_all, p_all, p_all, p_all, conv_w, wa_b, wc_b)


def _split_bf16(t):
    hi = t.astype(BF16)
    lo = (t - hi.astype(F32)).astype(BF16)
    return hi, lo


def _mix_route_body(m_ref, wm_ref, x_ref, mod_ref, gpost_ref, gpre_ref, wr_ref, br_ref,
                    h_ref, f_ref, route_ref, cnt_ref, carry_scr, f0_scr, f1_scr, *, n_groups, per_group, n_tiles):
    i = pl.program_id(0)
    f_slots = (f0_scr, f1_scr)

    def project(f_scr, stages=()):
        m = m_ref[...]
        stages = iter(stages)
        width = wm_ref.shape[1] // ROUTE_STAGES
        ys = []
        for c in range(ROUTE_STAGES):
            ys.append(jnp.dot(m, wm_ref[:, c * width:(c + 1) * width], preferred_element_type=F32))
            next(stages, None)
        y = jnp.concatenate(ys, axis=1)
        h = x_ref[...] + mod_ref[0, 2:3, :] * _rms(y, gpost_ref[...])
        h_ref[...] = h
        f = _rms(h, gpre_ref[...]) * (1.0 + mod_ref[0, 4:5, :]) + mod_ref[0, 3:4, :]
        f_ref[...] = f
        f_scr[...] = f

    route = functools.partial(_route_stages, wr_ref, br_ref, route_ref, cnt_ref, carry_scr,
                              n_groups=n_groups, per_group=per_group)

    @pl.when(i == 0)
    def _():
        carry_scr[...] = jnp.zeros_like(carry_scr)
        project(f_slots[0])

    for parity in range(2):
        @pl.when((i % 2 == parity) & (i > 0) & (i < n_tiles))
        def _(parity=parity):
            project(f_slots[parity], route(f_slots[1 - parity]))

    @pl.when(i == n_tiles)
    def _():
        for _ in route(f_slots[(n_tiles - 1) % 2]):
            pass


ROUTE_STAGES = 4


def _route_stages(wr_ref, br_ref, route_ref, cnt_ref, carry_scr, f_scr, *, n_groups, per_group):
    f = f_scr[...]
    tm = f.shape[0]
    f_hi, f_lo = _split_bf16(f)
    w_hi, w_lo = _split_bf16(wr_ref[...])
    logits = (jnp.dot(f_hi, w_hi, preferred_element_type=F32)
              + jnp.dot(f_hi, w_lo, preferred_element_type=F32)
              + jnp.dot(f_lo, w_hi, preferred_element_type=F32)) + br_ref[...]
    yield

    lane = lax.broadcasted_iota(jnp.int32, logits.shape, 1)

    def first_argmax(vals):
        mx = jnp.max(vals, axis=-1, keepdims=True)
        idx = jnp.min(jnp.where(vals == mx, lane, LANES), axis=-1, keepdims=True)
        return mx, idx

    is_group = lane < n_groups
    g_max, g_idx = first_argmax(jnp.where(is_group, logits, NEG_BIG))
    g_den = jnp.sum(jnp.where(is_group, jnp.exp(logits - g_max), 0.0), axis=-1, keepdims=True)
    grp_p = 1.0 / g_den
    yield
    lo_lane = n_groups + g_idx * per_group
    e_logits = jnp.where((lane >= lo_lane) & (lane < lo_lane + per_group), logits, NEG_BIG)
    l1, i1 = first_argmax(e_logits)
    l2, i2 = first_argmax(jnp.where(lane == i1, NEG_BIG, e_logits))
    r = jnp.exp(l2 - l1)
    gate1 = grp_p / (1.0 + r)
    gate2 = grp_p * r / (1.0 + r)
    e1 = i1 - n_groups
    e2 = i2 - n_groups
    yield

    onehot = ((lane == e1) | (lane == e2)).astype(BF16)
    r_i = lax.broadcasted_iota(jnp.int32, (tm, tm), 0)
    c_i = lax.broadcasted_iota(jnp.int32, (tm, tm), 1)
    before = (c_i < r_i).astype(BF16)
    seen = jnp.dot(before, onehot, preferred_element_type=F32) + carry_scr[0:1, :]
    rank1 = jnp.sum(jnp.where(lane == e1, seen, 0.0), axis=-1, keepdims=True)
    rank2 = jnp.sum(jnp.where(lane == e2, seen, 0.0), axis=-1, keepdims=True)
    carry_scr[...] = carry_scr[...] + jnp.sum(onehot.astype(F32), axis=0, keepdims=True)
    cnt_ref[...] = carry_scr[...]

    route = jnp.where(lane == 0, e1.astype(F32), 0.0)
    route = jnp.where(lane == 1, e2.astype(F32), route)
    route = jnp.where(lane == 2, gate1, route)
    route = jnp.where(lane == 3, gate2, route)
    route = jnp.where(lane == 4, rank1, route)
    route = jnp.where(lane == 5, rank2, route)
    route_ref[...] = route
    yield


def _mix_route(merged, wm_b, x2, mod3, gpost, gpre, w_r, b_r, *, tm, seq, n_groups, per_group):
    rows, d = x2.shape
    per_batch = seq // tm
    n_tiles = rows // tm
    body = functools.partial(_mix_route_body, n_groups=n_groups, per_group=per_group, n_tiles=n_tiles)

    def tile(i):
        return jnp.minimum(i, n_tiles - 1)

    def routed(i):
        return jnp.maximum(i - 1, 0)

    return pl.pallas_call(
        body,
        grid=(n_tiles + 1,),
        in_specs=[pl.BlockSpec((tm, d), lambda i: (tile(i), 0)),
                  pl.BlockSpec(wm_b.shape, lambda i: (0, 0), pipeline_mode=pl.Buffered(1)),
                  pl.BlockSpec((tm, d), lambda i: (tile(i), 0)),
                  pl.BlockSpec((1, N_MOD, d), lambda i: (tile(i) // per_batch, 0, 0)),
                  pl.BlockSpec((1, d), lambda i: (0, 0)),
                  pl.BlockSpec((1, d), lambda i: (0, 0)),
                  pl.BlockSpec((d, LANES), lambda i: (0, 0)),
                  pl.BlockSpec((1, LANES), lambda i: (0, 0))],
        out_specs=[pl.BlockSpec((tm, d), lambda i: (tile(i), 0)),
                   pl.BlockSpec((tm, d), lambda i: (tile(i), 0)),
                   pl.BlockSpec((tm, LANES), lambda i: (routed(i), 0)),
                   pl.BlockSpec((8, LANES), lambda i: (0, 0))],
        out_shape=[jax.ShapeDtypeStruct((rows, d), F32),
                   jax.ShapeDtypeStruct((rows, d), F32),
                   jax.ShapeDtypeStruct((rows, LANES), F32),
                   jax.ShapeDtypeStruct((8, LANES), F32)],
        scratch_shapes=[pltpu.VMEM((8, LANES), F32), pltpu.VMEM((tm, d), F32), pltpu.VMEM((tm, d), F32)],
        compiler_params=_params(1),
        name="mix_route",
    )(merged, wm_b, x2, mod3, gpost, gpre, w_r, b_r)


DMA_ISSUE_UNROLL = 8


def _dispatch_body(d1_ref, d2_ref, f_ref, xs_hbm, stage, sem, *, n_steps):
    i = pl.program_id(0)
    tm = f_ref.shape[0]
    slot = i % 2

    def drain(s):
        for _ in range(TOP_K):
            pltpu.make_async_copy(stage.at[s], xs_hbm.at[pl.ds(0, tm)], sem.at[s]).wait()

    @pl.when(i >= 2)
    def _():
        drain(slot)

    stage[slot] = f_ref[...].reshape(stage.shape[1:])
    base = i * tm

    def body(r, carry):
        t = base + r
        src = stage.at[slot, r]
        pltpu.make_async_copy(src, xs_hbm.at[d1_ref[t]], sem.at[slot]).start()
        pltpu.make_async_copy(src, xs_hbm.at[d2_ref[t]], sem.at[slot]).start(priority=1)
        return carry

    lax.fori_loop(0, tm, body, 0, unroll=DMA_ISSUE_UNROLL)

    @pl.when(i == n_steps - 1)
    def _():
        drain(slot)
        if n_steps >= 2:
            drain(1 - slot)


def _dispatch(dest1, dest2, f, *, n_rows, tm):
    n_tok, d = f.shape
    n_steps = n_tok // tm
    body = functools.partial(_dispatch_body, n_steps=n_steps)
    grid_spec = pltpu.PrefetchScalarGridSpec(
        num_scalar_prefetch=2,
        grid=(n_steps,),
        in_specs=[pl.BlockSpec((tm, d), lambda i, d1, d2: (i, 0))],
        out_specs=pl.BlockSpec(memory_space=pl.ANY),
        scratch_shapes=[pltpu.VMEM((2, tm, d // LANES, LANES), f.dtype),
                        pltpu.SemaphoreType.DMA((2,))],
    )
    return pl.pallas_call(
        body,
        grid_spec=grid_spec,
        out_shape=jax.ShapeDtypeStruct((n_rows, d // LANES, LANES), f.dtype),
        compiler_params=_params(1),
        name="dispatch",
    )(dest1, dest2, f)


def _experts_body(be_ref, nu_ref, valid_ref, first_ref, next_ref, slot_ref,
                  x_ref, wg_hbm, wu_hbm, wd_hbm, o_ref, wg_f, wu_f, wd_f, wsem, wg_b, wu_b, wd_b):
    i = pl.program_id(0)
    weights = ((wg_hbm, wg_f, wg_b), (wu_hbm, wu_f, wu_b), (wd_hbm, wd_f, wd_b))

    def weight_copies(expert, slot):
        return [pltpu.make_async_copy(hbm.at[expert], ring.at[slot], wsem.at[slot, k])
                for k, (hbm, ring, _) in enumerate(weights)]

    @pl.when(i == 0)
    def _():
        for cp in weight_copies(be_ref[0], 0):
            cp.start()

    @pl.when(i < nu_ref[0])
    def _():
        @pl.when(first_ref[i] == 1)
        def _():
            slot = slot_ref[i]
            for cp in weight_copies(be_ref[i], slot):
                cp.wait()
            for _, ring, cast in weights:
                cast[...] = ring[slot].astype(BF16)

            @pl.when(next_ref[i] >= 0)
            def _():
                for cp in weight_copies(next_ref[i], 1 - slot):
                    cp.start()

        bm = x_ref.shape[0]
        xr = x_ref[...].reshape(bm, wg_b.shape[0])
        row = lax.broadcasted_iota(jnp.int32, xr.shape, 0)
        x = jnp.where(row < valid_ref[i], xr, 0.0).astype(BF16)
        g = jnp.dot(x, wg_b[...], preferred_element_type=F32)
        u = jnp.dot(x, wu_b[...], preferred_element_type=F32)
        hid = (g * jax.nn.sigmoid(g) * u).astype(BF16)
        o_ref[...] = jnp.dot(hid, wd_b[...], preferred_element_type=F32).reshape(o_ref.shape)


def _experts(block_expert, n_used, block_valid, block_first, block_next, block_slot, xs, w_gate, w_up, w_down, *, bm):
    n_blocks = block_expert.shape[0]
    _, d, de = w_gate.shape

    def row_block(i, be, nu, *_):
        return (jnp.minimum(i, nu[0] - 1), 0, 0)

    grid_spec = pltpu.PrefetchScalarGridSpec(
        num_scalar_prefetch=6,
        grid=(n_blocks,),
        in_specs=[pl.BlockSpec((bm, d // LANES, LANES), row_block),
                  pl.BlockSpec(memory_space=pl.ANY),
                  pl.BlockSpec(memory_space=pl.ANY),
                  pl.BlockSpec(memory_space=pl.ANY)],
        out_specs=pl.BlockSpec((bm, d // LANES, LANES), row_block),
        scratch_shapes=[pltpu.VMEM((2, d, de), F32),
                        pltpu.VMEM((2, d, de), F32),
                        pltpu.VMEM((2, de, d), F32),
                        pltpu.SemaphoreType.DMA((2, 3)),
                        pltpu.VMEM((d, de), BF16),
                        pltpu.VMEM((d, de), BF16),
                        pltpu.VMEM((de, d), BF16)],
    )
    return pl.pallas_call(
        _experts_body,
        grid_spec=grid_spec,
        out_shape=jax.ShapeDtypeStruct((n_blocks * bm, d // LANES, LANES), F32),
        compiler_params=_params(1),
        name="experts",
    )(block_expert, n_used, block_valid, block_first, block_next, block_slot, xs, w_gate, w_up, w_down)


def _combine_body(d1_ref, d2_ref, yb_hbm, route_ref, h_ref, mod_ref, g_ref, o_ref, ybuf, sem):
    i = pl.program_id(0)
    n = pl.num_programs(0)
    tm = h_ref.shape[0]

    def start_gather(blk, slot):
        base = blk * tm

        def body(r, carry):
            pltpu.make_async_copy(yb_hbm.at[d1_ref[base + r]], ybuf.at[slot, 0, r], sem.at[slot]).start()
            pltpu.make_async_copy(yb_hbm.at[d2_ref[base + r]], ybuf.at[slot, 1, r], sem.at[slot]).start(priority=1)
            return carry

        lax.fori_loop(0, tm, body, 0, unroll=DMA_ISSUE_UNROLL)

    def wait_gather(slot):
        for k in range(TOP_K):
            pltpu.make_async_copy(yb_hbm.at[pl.ds(0, tm)], ybuf.at[slot, k], sem.at[slot]).wait()

    @pl.when(i == 0)
    def _():
        start_gather(0, 0)

    @pl.when(i + 1 < n)
    def _():
        start_gather(i + 1, (i + 1) % 2)

    slot = i % 2
    wait_gather(slot)
    y = (route_ref[:, 2:3] * ybuf[slot, 0].reshape(h_ref.shape)
         + route_ref[:, 3:4] * ybuf[slot, 1].reshape(h_ref.shape))
    o_ref[...] = h_ref[...] + mod_ref[0, 5:6, :] * _rms(y, g_ref[...])


def _combine(dest1, dest2, yb, route, h, mod3, gpost, *, tm, seq):
    rows, d = h.shape
    per_batch = seq // tm
    grid_spec = pltpu.PrefetchScalarGridSpec(
        num_scalar_prefetch=2,
        grid=(rows // tm,),
        in_specs=[pl.BlockSpec(memory_space=pl.ANY),
                  pl.BlockSpec((tm, LANES), lambda i, d1, d2: (i, 0)),
                  pl.BlockSpec((tm, d), lambda i, d1, d2: (i, 0)),
                  pl.BlockSpec((1, N_MOD, d), lambda i, d1, d2: (i // per_batch, 0, 0)),
                  pl.BlockSpec((1, d), lambda i, d1, d2: (0, 0))],
        out_specs=pl.BlockSpec((tm, d), lambda i, d1, d2: (i, 0)),
        scratch_shapes=[pltpu.VMEM((2, TOP_K, tm, d // LANES, LANES), F32),
                        pltpu.SemaphoreType.DMA((2,))],
    )
    return pl.pallas_call(
        _combine_body,
        grid_spec=grid_spec,
        out_shape=jax.ShapeDtypeStruct((rows, d), F32),
        compiler_params=_params(1),
        name="combine",
    )(dest1, dest2, yb, route, h, mod3, gpost)


def _rope_tables(seq):
    rows = seq // GRID_W
    row = jnp.repeat(jnp.arange(rows, dtype=F32), GRID_W)
    col = jnp.tile(jnp.arange(GRID_W, dtype=F32), rows)
    pairs = HEAD_DIM // 4
    freqs = ROPE_THETA ** (-jnp.arange(pairs, dtype=F32) / pairs)
    ang = jnp.concatenate([row[:, None] * freqs, col[:, None] * freqs], axis=-1)
    cos, sin = jnp.cos(ang), jnp.sin(ang)
    return jnp.concatenate([cos, cos], axis=-1), jnp.concatenate([-sin, sin], axis=-1)


def _tile(limit, size):
    t = min(limit, size)
    assert size % t == 0, (limit, size)
    return t


def kernel(x, c, ctx, c_ctx, w_ada, b_ada, g_pre_mix, w_in, q_norm, k_norm, conv_w, w_attn_out, w_conv_out,
           w_mix_out, g_post_mix, g_pre_ffn, w_router_group, b_router_group, w_router_expert, b_router_expert,
           w_gate, w_up, w_down, g_post_ffn):
    assert w_ada.shape[0] == 1, "single-layer problem"
    batch, seq, d = x.shape
    n_ctx = ctx.shape[1]
    attn_width = w_attn_out.shape[1]
    conv_width = conv_w.shape[-1]
    in_width = w_in.shape[-1]
    kv_width = (in_width - attn_width - 3 * conv_width - 2 * d) // 2
    n_kv = kv_width // HEAD_DIM
    group = attn_width // kv_width
    conv_start = attn_width + 2 * kv_width
    gate_start = conv_start + 3 * conv_width
    n_groups, per_group = w_router_expert.shape[2], w_router_expert.shape[3]
    n_experts = n_groups * per_group
    assert n_groups + n_experts <= LANES and batch + 1 <= 8
    n_tok = batch * seq

    cc = jnp.concatenate([c, c_ctx[None, :], jnp.zeros((8 - batch - 1, d), F32)], axis=0)
    mod3 = _adaln(cc, w_ada[0], b_ada[0]).reshape(8, N_MOD, d)

    w_in_b = w_in[0].astype(BF16)
    cos2, sin2 = _rope_tables(seq)
    tn = _tile(PROJ_COL_TILE, kv_width)
    tm_in = _tile(PROJ_ROW_TILE, seq)
    x2 = x.reshape(n_tok, d)
    common = dict(tn=tn, attn_width=attn_width, kv_width=kv_width, gate_start_col=gate_start)
    per_batch_in = seq // tm_in
    n_sections = 2
    assert in_width % (n_sections * tn) == 0 and attn_width % (2 * kv_width) == 0
    p_all = _inproj(x2, mod3, g_pre_mix, w_in_b, cos2, sin2, q_norm, k_norm, tm=tm_in,
                    sec_width=in_width // n_sections, sec_off=0, n_sections=n_sections,
                    mod_row=lambda i: i // per_batch_in, pos_blocks=per_batch_in, use_rope=True, **common)
    tm_ctx = _tile(PROJ_ROW_TILE, n_ctx)
    p_ctx = _inproj(ctx.reshape(batch * n_ctx, d), mod3, g_pre_mix, w_in_b, cos2[:tm_ctx], sin2[:tm_ctx],
                    q_norm, k_norm, tm=tm_ctx, sec_width=2 * kv_width, sec_off=attn_width // (2 * kv_width),
                    n_sections=1, mod_row=lambda i: batch, pos_blocks=1, use_rope=False, **common)

    attn = _attention(p_all, p_ctx, batch=batch, seq=seq, n_ctx=n_ctx, n_kv=n_kv, group=group,
                      tq=_tile(ATTN_QUERY_TILE, seq))

    tm_tok = _tile(TOKEN_TILE, seq)
    tm_wide = _tile(2 * TOKEN_TILE, seq)
    merged = _merge(attn, p_all, conv_w[0], w_attn_out[0].astype(BF16), w_conv_out[0].astype(BF16),
                    tm=tm_wide, seq=seq, conv_width=conv_width, d=d,
                    conv_start_col=conv_start, gate_start_col=gate_start)

    pad = LANES - n_groups - n_experts
    w_r = jnp.concatenate([w_router_group[0], w_router_expert[0].reshape(d, n_experts), jnp.zeros((d, pad), F32)], axis=1)
    b_r = jnp.concatenate([b_router_group[0], b_router_expert[0].reshape(n_experts), jnp.zeros((pad,), F32)])[None, :]
    h, f, route, counts = _mix_route(merged, w_mix_out[0].astype(BF16), x2, mod3, g_post_mix, g_pre_ffn, w_r, b_r,
                                     tm=tm_tok, seq=seq, n_groups=n_groups, per_group=per_group)

    bm = _tile(EXPERT_BLOCK_ROWS, n_tok)
    n_blocks = n_tok * TOP_K // bm + n_experts
    cnt = counts[0, :n_experts].astype(jnp.int32)
    padded = (cnt + bm - 1) // bm * bm
    pend = jnp.cumsum(padded)
    pstart = pend - padded
    e1, e2 = route[:, 0].astype(jnp.int32), route[:, 1].astype(jnp.int32)
    expert_ids = jnp.arange(n_experts, dtype=jnp.int32)[None, :]

    def segment_start(e):
        return jnp.sum(jnp.where(e[:, None] == expert_ids, pstart[None, :], 0), axis=1)

    dest1 = segment_start(e1) + route[:, 4].astype(jnp.int32)
    dest2 = segment_start(e2) + route[:, 5].astype(jnp.int32)
    n_used = (pend[-1] // bm).astype(jnp.int32)
    blk_row = jnp.minimum(jnp.arange(n_blocks, dtype=jnp.int32), n_used - 1) * bm
    blk_exp = jnp.sum((pend[None, :] <= blk_row[:, None]).astype(jnp.int32), axis=1)
    blk_exp = jnp.minimum(blk_exp, n_experts - 1)

    def per_block(table):
        return jnp.sum(jnp.where(blk_exp[:, None] == expert_ids, table[None, :], 0), axis=1)

    blk_valid = jnp.clip(per_block(pstart + cnt) - blk_row, 0, bm)
    blk_first = (blk_row == per_block(pstart)).astype(jnp.int32)
    used = cnt > 0
    later = jnp.where(used[None, :] & (expert_ids > expert_ids.T), expert_ids, n_experts)
    next_used = jnp.min(later, axis=1)
    blk_next = per_block(jnp.where(next_used < n_experts, next_used, -1))
    blk_slot = per_block((jnp.cumsum(used.astype(jnp.int32)) - 1) % 2)

    xs = _dispatch(dest1, dest2, f, n_rows=n_blocks * bm, tm=tm_wide)
    yb = _experts(blk_exp, n_used.reshape(1), blk_valid, blk_first, blk_next, blk_slot, xs,
                  w_gate[0], w_up[0], w_down[0], bm=bm)
    out = _combine(dest1, dest2, yb, route, h, mod3, g_post_ffn, tm=tm_wide, seq=seq)
    return out.reshape(batch, seq, d)
```

```python
import functools

import jax
import jax.numpy as jnp
from jax import lax
from jax.experimental import pallas as pl
from jax.experimental.pallas import tpu as pltpu

HEAD_DIM = 128
GRID_W = 64
ROPE_THETA = 10000.0
EPS = 1e-6
TOP_K = 2
N_MOD = 6
LANES = 128
BF16_SUBLANES = 16
VMEM_LIMIT_BYTES = 56 * 1024 * 1024
NEG_BIG = -1e30
LOG2_E = 1.4426950408889634

PROJ_ROW_TILE = 512
PROJ_COL_TILE = 512
ATTN_QUERY_TILE = 128
TOKEN_TILE = 256
EXPERT_BLOCK_ROWS = 256

F32 = jnp.float32
BF16 = jnp.bfloat16


def _params(n_grid_axes):
    return pltpu.CompilerParams(dimension_semantics=("arbitrary",) * n_grid_axes,
                                vmem_limit_bytes=VMEM_LIMIT_BYTES)


def _rms(t, gain):
    ms = jnp.mean(t * t, axis=-1, keepdims=True)
    return t * lax.rsqrt(ms + EPS) * gain


def _adaln_body(c_ref, w_ref, b_ref, o_ref):
    c = c_ref[...]
    s = (c * jax.nn.sigmoid(c)).astype(BF16)
    o_ref[...] = jnp.dot(s, w_ref[...].astype(BF16), preferred_element_type=F32) + b_ref[...]


def _adaln(cc, w, b):
    rows, d = cc.shape
    n = w.shape[1]
    tn = min(1024, n)
    return pl.pallas_call(
        _adaln_body,
        grid=(n // tn,),
        in_specs=[pl.BlockSpec((rows, d), lambda j: (0, 0)),
                  pl.BlockSpec((d, tn), lambda j: (0, j)),
                  pl.BlockSpec((1, tn), lambda j: (0, j))],
        out_specs=pl.BlockSpec((rows, tn), lambda j: (0, j)),
        out_shape=jax.ShapeDtypeStruct((rows, n), F32),
        compiler_params=_params(1),
        name="adaln",
    )(cc, w, b.reshape(1, n))


Q_COLS, K_COLS, PLAIN_COLS, GATE_COLS = "q", "k", "plain", "gate"


def _inproj_body(x_ref, mod_ref, g_ref, w_ref, cos_ref, sin_ref, qn_ref, kn_ref, o_ref, *,
                 sections, tn, use_rope, q_scale):
    sec = pl.program_id(0)
    tm = x_ref.shape[0]
    half = tm // 2

    def normalised(r0):
        xn = _rms(x_ref[r0:r0 + half, :], g_ref[...])
        return (xn * (1.0 + mod_ref[0, 1:2, :]) + mod_ref[0, 0:1, :]).astype(BF16)

    def normed_heads(acc, r0, col0, gain, out_scale):
        for hh in range(tn // HEAD_DIM):
            t = _rms(acc[:, hh * HEAD_DIM:(hh + 1) * HEAD_DIM], gain)
            if use_rope:
                t = t * cos_ref[r0:r0 + half, :] + pltpu.roll(t, HEAD_DIM // 2, 1) * sin_ref[r0:r0 + half, :]
            if out_scale != 1.0:
                t = t * out_scale
            o_ref[r0:r0 + half, col0 + hh * HEAD_DIM:col0 + (hh + 1) * HEAD_DIM] = t.astype(o_ref.dtype)

    def column_tile(a, r0, col0, kind):
        acc = jnp.dot(a, w_ref[:, col0:col0 + tn], preferred_element_type=F32)
        if kind == Q_COLS:
            normed_heads(acc, r0, col0, qn_ref[...], q_scale)
        elif kind == K_COLS:
            normed_heads(acc, r0, col0, kn_ref[...], 1.0)
        elif kind == PLAIN_COLS:
            o_ref[r0:r0 + half, col0:col0 + tn] = acc.astype(o_ref.dtype)
        else:
            o_ref[r0:r0 + half, col0:col0 + tn] = jax.nn.sigmoid(acc).astype(o_ref.dtype)

    def section(kinds):
        a_lo = normalised(0)
        column_tile(a_lo, 0, 0, kinds[0])
        a_hi = normalised(half)
        column_tile(a_hi, half, 0, kinds[0])
        for t, kind in enumerate(kinds[1:], start=1):
            column_tile(a_lo, 0, t * tn, kind)
            column_tile(a_hi, half, t * tn, kind)

    for s_idx, kinds in enumerate(sections):
        pl.when(sec == s_idx)(functools.partial(section, kinds))


def _inproj(x2, mod3, g, w_b, cos2, sin2, qn, kn, *, tm, tn, sec_width, sec_off, n_sections, mod_row, pos_blocks,
            attn_width, kv_width, gate_start_col, use_rope):
    rows, d = x2.shape

    def kind_of(col):
        if col < attn_width:
            return Q_COLS
        if col < attn_width + kv_width:
            return K_COLS
        return PLAIN_COLS if col < gate_start_col else GATE_COLS

    sections = tuple(tuple(kind_of((sec_off + s) * sec_width + t * tn) for t in range(sec_width // tn))
                     for s in range(n_sections))
    body = functools.partial(_inproj_body, sections=sections, tn=tn, use_rope=use_rope,
                             q_scale=HEAD_DIM ** -0.5 * LOG2_E)
    return pl.pallas_call(
        body,
        grid=(n_sections, rows // tm),
        in_specs=[pl.BlockSpec((tm, d), lambda s, i: (i, 0)),
                  pl.BlockSpec((1, N_MOD, d), lambda s, i: (mod_row(i), 0, 0)),
                  pl.BlockSpec((1, d), lambda s, i: (0, 0)),
                  pl.BlockSpec((d, sec_width), lambda s, i: (0, s + sec_off), pipeline_mode=pl.Buffered(1)),
                  pl.BlockSpec((tm, HEAD_DIM), lambda s, i: (i % pos_blocks, 0)),
                  pl.BlockSpec((tm, HEAD_DIM), lambda s, i: (i % pos_blocks, 0)),
                  pl.BlockSpec((1, HEAD_DIM), lambda s, i: (0, 0)),
                  pl.BlockSpec((1, HEAD_DIM), lambda s, i: (0, 0))],
        out_specs=pl.BlockSpec((tm, sec_width), lambda s, i: (i, s)),
        out_shape=jax.ShapeDtypeStruct((rows, n_sections * sec_width), BF16),
        compiler_params=_params(2),
        name="inproj_rope" if use_rope else "inproj_ctx",
    )(x2, mod3, g, w_b, cos2, sin2, qn, kn)


ATTN_KEY_CHUNK = 256


def _attn_body(q_ref, k_ref, v_ref, kc_ref, vc_ref, o_ref, s0_scr, m0_scr, s1_scr, m1_scr, vt_scr, *, group, tq):
    seq = q_ref.shape[0]
    q_tiles = seq // tq
    slabs = ((s0_scr, m0_scr), (s1_scr, m1_scr))
    vt_scr[:, :seq] = v_ref[...].astype(F32).T.astype(BF16)
    vt_scr[:, seq:] = vc_ref[...].astype(F32).T.astype(BF16)

    def rows(t):
        start = t * tq
        return pl.ds(start if isinstance(start, int) else pl.multiple_of(start, tq), tq)

    n_ctx = kc_ref.shape[0]
    chunk = min(ATTN_KEY_CHUNK, seq)
    key_chunks = [(k_ref, c0, c0, chunk) for c0 in range(0, seq, chunk)] + [(kc_ref, 0, seq, n_ctx)]

    def step(t, score_slab, finish_slab):
        if score_slab is not None:
            q = jnp.concatenate([q_ref[rows(t), g * HEAD_DIM:(g + 1) * HEAD_DIM] for g in range(group)], axis=0)
            m_run = None
        if finish_slab is not None:
            m_prev = finish_slab[1][...]
            l = 0.0
            acc = 0.0
        for keys, k0, s0, size in key_chunks:
            if score_slab is not None:
                s = lax.dot_general(keys[k0:k0 + size, :], q, (((1,), (1,)), ((), ())),
                                    preferred_element_type=F32)
                score_slab[0][s0:s0 + size] = s
                m_chunk = jnp.max(s, axis=0, keepdims=True)
                m_run = m_chunk if m_run is None else jnp.maximum(m_run, m_chunk)
            if finish_slab is not None:
                p = jnp.exp2(finish_slab[0][s0:s0 + size] - m_prev)
                l = l + jnp.sum(p, axis=0, keepdims=True)
                acc = acc + jnp.dot(vt_scr[:, s0:s0 + size], p.astype(BF16), preferred_element_type=F32)
        if score_slab is not None:
            score_slab[1][...] = m_run
        if finish_slab is not None:
            out = acc * (1.0 / l)
            for g in range(group):
                o_ref[rows(t - 1), g * HEAD_DIM:(g + 1) * HEAD_DIM] = (
                    out[:, g * tq:(g + 1) * tq].T.astype(o_ref.dtype))

    step(0, slabs[0], None)
    n_pairs = (q_tiles - 1) // 2

    def pair(j, carry):
        t = 1 + 2 * j
        step(t, slabs[1], slabs[0])
        step(t + 1, slabs[0], slabs[1])
        return carry

    if n_pairs > 0:
        lax.fori_loop(0, n_pairs, pair, 0)
    last = q_tiles - 1
    if last % 2:
        step(last, slabs[1], slabs[0])
    step(q_tiles, None, slabs[last % 2])


def _attention(p_all, p_ctx, *, batch, seq, n_ctx, n_kv, group, tq):
    gw = group * HEAD_DIM
    body = functools.partial(_attn_body, group=group, tq=tq)
    n_keys = seq + n_ctx
    k_col = n_kv * group
    v_col = k_col + n_kv
    return pl.pallas_call(
        body,
        grid=(batch, n_kv),
        in_specs=[pl.BlockSpec((seq, gw), lambda b, h: (b, h)),
                  pl.BlockSpec((seq, HEAD_DIM), lambda b, h: (b, k_col + h)),
                  pl.BlockSpec((seq, HEAD_DIM), lambda b, h: (b, v_col + h)),
                  pl.BlockSpec((n_ctx, HEAD_DIM), lambda b, h: (b, h)),
                  pl.BlockSpec((n_ctx, HEAD_DIM), lambda b, h: (b, n_kv + h))],
        out_specs=pl.BlockSpec((seq, gw), lambda b, h: (b, h)),
        out_shape=jax.ShapeDtypeStruct((batch * seq, n_kv * gw), BF16),
        scratch_shapes=[pltpu.VMEM((n_keys, group * tq), F32), pltpu.VMEM((1, group * tq), F32)] * 2
        + [pltpu.VMEM((HEAD_DIM, n_keys), BF16)],
        compiler_params=_params(2),
        name="attention",
    )(p_all, p_all, p_all, p_ctx, p_ctx)


MERGE_CHUNKS = 4


def _merge_body(attn_ref, pb_ref, pc_ref, px_ref, pcp_ref, pxp_ref, pcn_ref, pxn_ref, ga_ref, gc_ref,
                cw_ref, wa_ref, wc_ref, o_ref, *, tiles_per_seq):
    i = pl.program_id(0)
    tm = attn_ref.shape[0]
    attn = attn_ref[...]
    width = o_ref.shape[1] // MERGE_CHUNKS
    y_attn0 = jnp.dot(attn, wa_ref[:, :width], preferred_element_type=F32)

    u = pc_ref[...].astype(F32) * px_ref[...].astype(F32)
    last = BF16_SUBLANES - 1
    u_before = pcp_ref[last:last + 1, :].astype(F32) * pxp_ref[last:last + 1, :].astype(F32)
    u_after = pcn_ref[0:1, :].astype(F32) * pxn_ref[0:1, :].astype(F32)
    pos = i % tiles_per_seq
    u_before = jnp.where(pos == 0, 0.0, u_before)
    u_after = jnp.where(pos == tiles_per_seq - 1, 0.0, u_after)
    row = lax.broadcasted_iota(jnp.int32, u.shape, 0)
    u_prev = jnp.where(row == 0, u_before, pltpu.roll(u, 1, 0))
    u_next = jnp.where(row == tm - 1, u_after, pltpu.roll(u, tm - 1, 0))
    conv = cw_ref[0:1, :] * u_prev + cw_ref[1:2, :] * u + cw_ref[2:3, :] * u_next
    z = (pb_ref[...].astype(F32) * conv).astype(BF16)

    for c in range(MERGE_CHUNKS):
        cols = slice(c * width, (c + 1) * width)
        y_attn = y_attn0 if c == 0 else jnp.dot(attn, wa_ref[:, cols], preferred_element_type=F32)
        y_conv = jnp.dot(z, wc_ref[:, cols], preferred_element_type=F32)
        merged = ga_ref[:, cols].astype(F32) * y_attn + gc_ref[:, cols].astype(F32) * y_conv
        o_ref[:, cols] = merged.astype(o_ref.dtype)


def _merge(attn, p_all, conv_w, wa_b, wc_b, *, tm, seq, conv_width, d, conv_start_col, gate_start_col):
    rows, attn_width = attn.shape
    hb = BF16_SUBLANES
    cb = conv_start_col // conv_width
    gb = gate_start_col // d
    n_hblocks = rows // hb
    per = tm // hb

    def prev_map(col):
        return lambda i: (jnp.maximum(i * per - 1, 0), col)

    def next_map(col):
        return lambda i: (jnp.minimum((i + 1) * per, n_hblocks - 1), col)

    body = functools.partial(_merge_body, tiles_per_seq=seq // tm)
    resident = dict(pipeline_mode=pl.Buffered(1))
    return pl.pallas_call(
        body,
        grid=(rows // tm,),
        in_specs=[pl.BlockSpec((tm, attn_width), lambda i: (i, 0)),
                  pl.BlockSpec((tm, conv_width), lambda i: (i, cb)),
                  pl.BlockSpec((tm, conv_width), lambda i: (i, cb + 1)),
                  pl.BlockSpec((tm, conv_width), lambda i: (i, cb + 2)),
                  pl.BlockSpec((hb, conv_width), prev_map(cb + 1)),
                  pl.BlockSpec((hb, conv_width), prev_map(cb + 2)),
                  pl.BlockSpec((hb, conv_width), next_map(cb + 1)),
                  pl.BlockSpec((hb, conv_width), next_map(cb + 2)),
                  pl.BlockSpec((tm, d), lambda i: (i, gb)),
                  pl.BlockSpec((tm, d), lambda i: (i, gb + 1)),
                  pl.BlockSpec(conv_w.shape, lambda i: (0, 0)),
                  pl.BlockSpec(wa_b.shape, lambda i: (0, 0), **resident),
                  pl.BlockSpec(wc_b.shape, lambda i: (0, 0), **resident)],
        out_specs=pl.BlockSpec((tm, d), lambda i: (i, 0)),
        out_shape=jax.ShapeDtypeStruct((rows, d), BF16),
        compiler_params=_params(1),
        name="merge",
    )(attn, p_all, p_all, p_all, p_all, p_all, p_all, p_all, p_all, p_all, conv_w, wa_b, wc_b)


def _split_bf16(t):
    hi = t.astype(BF16)
    lo = (t - hi.astype(F32)).astype(BF16)
    return hi, lo


def _mix_route_body(m_ref, wm_ref, x_ref, mod_ref, gpost_ref, gpre_ref, wr_ref, br_ref,
                    h_ref, f_ref, route_ref, cnt_ref, carry_scr, f0_scr, f1_scr, *, n_groups, per_group, n_tiles):
    i = pl.program_id(0)
    f_slots = (f0_scr, f1_scr)

    def project(f_scr, stages=()):
        m = m_ref[...]
        stages = iter(stages)
        width = wm_ref.shape[1] // ROUTE_STAGES
        ys = []
        for c in range(ROUTE_STAGES):
            ys.append(jnp.dot(m, wm_ref[:, c * width:(c + 1) * width], preferred_element_type=F32))
            next(stages, None)
        y = jnp.concatenate(ys, axis=1)
        h = x_ref[...] + mod_ref[0, 2:3, :] * _rms(y, gpost_ref[...])
        h_ref[...] = h
        f = _rms(h, gpre_ref[...]) * (1.0 + mod_ref[0, 4:5, :]) + mod_ref[0, 3:4, :]
        f_ref[...] = f
        f_scr[...] = f

    route = functools.partial(_route_stages, wr_ref, br_ref, route_ref, cnt_ref, carry_scr,
                              n_groups=n_groups, per_group=per_group)

    @pl.when(i == 0)
    def _():
        carry_scr[...] = jnp.zeros_like(carry_scr)
        project(f_slots[0])

    for parity in range(2):
        @pl.when((i % 2 == parity) & (i > 0) & (i < n_tiles))
        def _(parity=parity):
            project(f_slots[parity], route(f_slots[1 - parity]))

    @pl.when(i == n_tiles)
    def _():
        for _ in route(f_slots[(n_tiles - 1) % 2]):
            pass


ROUTE_STAGES = 4


def _route_stages(wr_ref, br_ref, route_ref, cnt_ref, carry_scr, f_scr, *, n_groups, per_group):
    f = f_scr[...]
    tm = f.shape[0]
    f_hi, f_lo = _split_bf16(f)
    w_hi, w_lo = _split_bf16(wr_ref[...])
    logits = (jnp.dot(f_hi, w_hi, preferred_element_type=F32)
              + jnp.dot(f_hi, w_lo, preferred_element_type=F32)
              + jnp.dot(f_lo, w_hi, preferred_element_type=F32)) + br_ref[...]
    yield

    lane = lax.broadcasted_iota(jnp.int32, logits.shape, 1)

    def first_argmax(vals):
        mx = jnp.max(vals, axis=-1, keepdims=True)
        idx = jnp.min(jnp.where(vals == mx, lane, LANES), axis=-1, keepdims=True)
        return mx, idx

    is_group = lane < n_groups
    g_max, g_idx = first_argmax(jnp.where(is_group, logits, NEG_BIG))
    g_den = jnp.sum(jnp.where(is_group, jnp.exp(logits - g_max), 0.0), axis=-1, keepdims=True)
    grp_p = 1.0 / g_den
    yield
    lo_lane = n_groups + g_idx * per_group
    e_logits = jnp.where((lane >= lo_lane) & (lane < lo_lane + per_group), logits, NEG_BIG)
    l1, i1 = first_argmax(e_logits)
    l2, i2 = first_argmax(jnp.where(lane == i1, NEG_BIG, e_logits))
    r = jnp.exp(l2 - l1)
    gate1 = grp_p / (1.0 + r)
    gate2 = grp_p * r / (1.0 + r)
    e1 = i1 - n_groups
    e2 = i2 - n_groups
    yield

    onehot = ((lane == e1) | (lane == e2)).astype(BF16)
    r_i = lax.broadcasted_iota(jnp.int32, (tm, tm), 0)
    c_i = lax.broadcasted_iota(jnp.int32, (tm, tm), 1)
    before = (c_i < r_i).astype(BF16)
    seen = jnp.dot(before, onehot, preferred_element_type=F32) + carry_scr[0:1, :]
    rank1 = jnp.sum(jnp.where(lane == e1, seen, 0.0), axis=-1, keepdims=True)
    rank2 = jnp.sum(jnp.where(lane == e2, seen, 0.0), axis=-1, keepdims=True)
    carry_scr[...] = carry_scr[...] + jnp.sum(onehot.astype(F32), axis=0, keepdims=True)
    cnt_ref[...] = carry_scr[...]

    route = jnp.where(lane == 0, e1.astype(F32), 0.0)
    route = jnp.where(lane == 1, e2.astype(F32), route)
    route = jnp.where(lane == 2, gate1, route)
    route = jnp.where(lane == 3, gate2, route)
    route = jnp.where(lane == 4, rank1, route)
    route = jnp.where(lane == 5, rank2, route)
    route_ref[...] = route
    yield


def _mix_route(merged, wm_b, x2, mod3, gpost, gpre, w_r, b_r, *, tm, seq, n_groups, per_group):
    rows, d = x2.shape
    per_batch = seq // tm
    n_tiles = rows // tm
    body = functools.partial(_mix_route_body, n_groups=n_groups, per_group=per_group, n_tiles=n_tiles)

    def tile(i):
        return jnp.minimum(i, n_tiles - 1)

    def routed(i):
        return jnp.maximum(i - 1, 0)

    return pl.pallas_call(
        body,
        grid=(n_tiles + 1,),
        in_specs=[pl.BlockSpec((tm, d), lambda i: (tile(i), 0)),
                  pl.BlockSpec(wm_b.shape, lambda i: (0, 0), pipeline_mode=pl.Buffered(1)),
                  pl.BlockSpec((tm, d), lambda i: (tile(i), 0)),
                  pl.BlockSpec((1, N_MOD, d), lambda i: (tile(i) // per_batch, 0, 0)),
                  pl.BlockSpec((1, d), lambda i: (0, 0)),
                  pl.BlockSpec((1, d), lambda i: (0, 0)),
                  pl.BlockSpec((d, LANES), lambda i: (0, 0)),
                  pl.BlockSpec((1, LANES), lambda i: (0, 0))],
        out_specs=[pl.BlockSpec((tm, d), lambda i: (tile(i), 0)),
                   pl.BlockSpec((tm, d), lambda i: (tile(i), 0)),
                   pl.BlockSpec((tm, LANES), lambda i: (routed(i), 0)),
                   pl.BlockSpec((8, LANES), lambda i: (0, 0))],
        out_shape=[jax.ShapeDtypeStruct((rows, d), F32),
                   jax.ShapeDtypeStruct((rows, d), F32),
                   jax.ShapeDtypeStruct((rows, LANES), F32),
                   jax.ShapeDtypeStruct((8, LANES), F32)],
        scratch_shapes=[pltpu.VMEM((8, LANES), F32), pltpu.VMEM((tm, d), F32), pltpu.VMEM((tm, d), F32)],
        compiler_params=_params(1),
        name="mix_route",
    )(merged, wm_b, x2, mod3, gpost, gpre, w_r, b_r)


DMA_ISSUE_UNROLL = 8


def _dispatch_body(d1_ref, d2_ref, f_ref, xs_hbm, stage, sem, *, n_steps):
    i = pl.program_id(0)
    tm = f_ref.shape[0]
    slot = i % 2

    def drain(s):
        for _ in range(TOP_K):
            pltpu.make_async_copy(stage.at[s], xs_hbm.at[pl.ds(0, tm)], sem.at[s]).wait()

    @pl.when(i >= 2)
    def _():
        drain(slot)

    stage[slot] = f_ref[...].reshape(stage.shape[1:])
    base = i * tm

    def body(r, carry):
        t = base + r
        src = stage.at[slot, r]
        pltpu.make_async_copy(src, xs_hbm.at[d1_ref[t]], sem.at[slot]).start()
        pltpu.make_async_copy(src, xs_hbm.at[d2_ref[t]], sem.at[slot]).start(priority=1)
        return carry

    lax.fori_loop(0, tm, body, 0, unroll=DMA_ISSUE_UNROLL)

    @pl.when(i == n_steps - 1)
    def _():
        drain(slot)
        if n_steps >= 2:
            drain(1 - slot)


def _dispatch(dest1, dest2, f, *, n_rows, tm):
    n_tok, d = f.shape
    n_steps = n_tok // tm
    body = functools.partial(_dispatch_body, n_steps=n_steps)
    grid_spec = pltpu.PrefetchScalarGridSpec(
        num_scalar_prefetch=2,
        grid=(n_steps,),
        in_specs=[pl.BlockSpec((tm, d), lambda i, d1, d2: (i, 0))],
        out_specs=pl.BlockSpec(memory_space=pl.ANY),
        scratch_shapes=[pltpu.VMEM((2, tm, d // LANES, LANES), f.dtype),
                        pltpu.SemaphoreType.DMA((2,))],
    )
    return pl.pallas_call(
        body,
        grid_spec=grid_spec,
        out_shape=jax.ShapeDtypeStruct((n_rows, d // LANES, LANES), f.dtype),
        compiler_params=_params(1),
        name="dispatch",
    )(dest1, dest2, f)


def _experts_body(be_ref, nu_ref, valid_ref, first_ref, next_ref, slot_ref,
                  x_ref, wg_hbm, wu_hbm, wd_hbm, o_ref, wg_f, wu_f, wd_f, wsem, wg_b, wu_b, wd_b):
    i = pl.program_id(0)
    weights = ((wg_hbm, wg_f, wg_b), (wu_hbm, wu_f, wu_b), (wd_hbm, wd_f, wd_b))

    def weight_copies(expert, slot):
        return [pltpu.make_async_copy(hbm.at[expert], ring.at[slot], wsem.at[slot, k])
                for k, (hbm, ring, _) in enumerate(weights)]

    @pl.when(i == 0)
    def _():
        for cp in weight_copies(be_ref[0], 0):
            cp.start()

    @pl.when(i < nu_ref[0])
    def _():
        @pl.when(first_ref[i] == 1)
        def _():
            slot = slot_ref[i]
            for cp in weight_copies(be_ref[i], slot):
                cp.wait()
            for _, ring, cast in weights:
                cast[...] = ring[slot].astype(BF16)

            @pl.when(next_ref[i] >= 0)
            def _():
                for cp in weight_copies(next_ref[i], 1 - slot):
                    cp.start()

        bm = x_ref.shape[0]
        xr = x_ref[...].reshape(bm, wg_b.shape[0])
        row = lax.broadcasted_iota(jnp.int32, xr.shape, 0)
        x = jnp.where(row < valid_ref[i], xr, 0.0).astype(BF16)
        g = jnp.dot(x, wg_b[...], preferred_element_type=F32)
        u = jnp.dot(x, wu_b[...], preferred_element_type=F32)
        hid = (g * jax.nn.sigmoid(g) * u).astype(BF16)
        o_ref[...] = jnp.dot(hid, wd_b[...], preferred_element_type=F32).reshape(o_ref.shape)


def _experts(block_expert, n_used, block_valid, block_first, block_next, block_slot, xs, w_gate, w_up, w_down, *, bm):
    n_blocks = block_expert.shape[0]
    _, d, de = w_gate.shape

    def row_block(i, be, nu, *_):
        return (jnp.minimum(i, nu[0] - 1), 0, 0)

    grid_spec = pltpu.PrefetchScalarGridSpec(
        num_scalar_prefetch=6,
        grid=(n_blocks,),
        in_specs=[pl.BlockSpec((bm, d // LANES, LANES), row_block),
                  pl.BlockSpec(memory_space=pl.ANY),
                  pl.BlockSpec(memory_space=pl.ANY),
                  pl.BlockSpec(memory_space=pl.ANY)],
        out_specs=pl.BlockSpec((bm, d // LANES, LANES), row_block),
        scratch_shapes=[pltpu.VMEM((2, d, de), F32),
                        pltpu.VMEM((2, d, de), F32),
                        pltpu.VMEM((2, de, d), F32),
                        pltpu.SemaphoreType.DMA((2, 3)),
                        pltpu.VMEM((d, de), BF16),
                        pltpu.VMEM((d, de), BF16),
                        pltpu.VMEM((de, d), BF16)],
    )
    return pl.pallas_call(
        _experts_body,
        grid_spec=grid_spec,
        out_shape=jax.ShapeDtypeStruct((n_blocks * bm, d // LANES, LANES), F32),
        compiler_params=_params(1),
        name="experts",
    )(block_expert, n_used, block_valid, block_first, block_next, block_slot, xs, w_gate, w_up, w_down)


def _combine_body(d1_ref, d2_ref, yb_hbm, route_ref, h_ref, mod_ref, g_ref, o_ref, ybuf, sem):
    i = pl.program_id(0)
    n = pl.num_programs(0)
    tm = h_ref.shape[0]

    def start_gather(blk, slot):
        base = blk * tm

        def body(r, carry):
            pltpu.make_async_copy(yb_hbm.at[d1_ref[base + r]], ybuf.at[slot, 0, r], sem.at[slot]).start()
            pltpu.make_async_copy(yb_hbm.at[d2_ref[base + r]], ybuf.at[slot, 1, r], sem.at[slot]).start(priority=1)
            return carry

        lax.fori_loop(0, tm, body, 0, unroll=DMA_ISSUE_UNROLL)

    def wait_gather(slot):
        for k in range(TOP_K):
            pltpu.make_async_copy(yb_hbm.at[pl.ds(0, tm)], ybuf.at[slot, k], sem.at[slot]).wait()

    @pl.when(i == 0)
    def _():
        start_gather(0, 0)

    @pl.when(i + 1 < n)
    def _():
        start_gather(i + 1, (i + 1) % 2)

    slot = i % 2
    wait_gather(slot)
    y = (route_ref[:, 2:3] * ybuf[slot, 0].reshape(h_ref.shape)
         + route_ref[:, 3:4] * ybuf[slot, 1].reshape(h_ref.shape))
    o_ref[...] = h_ref[...] + mod_ref[0, 5:6, :] * _rms(y, g_ref[...])


def _combine(dest1, dest2, yb, route, h, mod3, gpost, *, tm, seq):
    rows, d = h.shape
    per_batch = seq // tm
    grid_spec = pltpu.PrefetchScalarGridSpec(
        num_scalar_prefetch=2,
        grid=(rows // tm,),
        in_specs=[pl.BlockSpec(memory_space=pl.ANY),
                  pl.BlockSpec((tm, LANES), lambda i, d1, d2: (i, 0)),
                  pl.BlockSpec((tm, d), lambda i, d1, d2: (i, 0)),
                  pl.BlockSpec((1, N_MOD, d), lambda i, d1, d2: (i // per_batch, 0, 0)),
                  pl.BlockSpec((1, d), lambda i, d1, d2: (0, 0))],
        out_specs=pl.BlockSpec((tm, d), lambda i, d1, d2: (i, 0)),
        scratch_shapes=[pltpu.VMEM((2, TOP_K, tm, d // LANES, LANES), F32),
                        pltpu.SemaphoreType.DMA((2,))],
    )
    return pl.pallas_call(
        _combine_body,
        grid_spec=grid_spec,
        out_shape=jax.ShapeDtypeStruct((rows, d), F32),
        compiler_params=_params(1),
        name="combine",
    )(dest1, dest2, yb, route, h, mod3, gpost)


def _rope_tables(seq):
    rows = seq // GRID_W
    row = jnp.repeat(jnp.arange(rows, dtype=F32), GRID_W)
    col = jnp.tile(jnp.arange(GRID_W, dtype=F32), rows)
    pairs = HEAD_DIM // 4
    freqs = ROPE_THETA ** (-jnp.arange(pairs, dtype=F32) / pairs)
    ang = jnp.concatenate([row[:, None] * freqs, col[:, None] * freqs], axis=-1)
    cos, sin = jnp.cos(ang), jnp.sin(ang)
    return jnp.concatenate([cos, cos], axis=-1), jnp.concatenate([-sin, sin], axis=-1)


def _tile(limit, size):
    t = min(limit, size)
    assert size % t == 0, (limit, size)
    return t


def kernel(x, c, ctx, c_ctx, w_ada, b_ada, g_pre_mix, w_in, q_norm, k_norm, conv_w, w_attn_out, w_conv_out,
           w_mix_out, g_post_mix, g_pre_ffn, w_router_group, b_router_group, w_router_expert, b_router_expert,
           w_gate, w_up, w_down, g_post_ffn):
    assert w_ada.shape[0] == 1, "single-layer problem"
    batch, seq, d = x.shape
    n_ctx = ctx.shape[1]
    attn_width = w_attn_out.shape[1]
    conv_width = conv_w.shape[-1]
    in_width = w_in.shape[-1]
    kv_width = (in_width - attn_width - 3 * conv_width - 2 * d) // 2
    n_kv = kv_width // HEAD_DIM
    group = attn_width // kv_width
    conv_start = attn_width + 2 * kv_width
    gate_start = conv_start + 3 * conv_width
    n_groups, per_group = w_router_expert.shape[2], w_router_expert.shape[3]
    n_experts = n_groups * per_group
    assert n_groups + n_experts <= LANES and batch + 1 <= 8
    n_tok = batch * seq

    cc = jnp.concatenate([c, c_ctx[None, :], jnp.zeros((8 - batch - 1, d), F32)], axis=0)
    mod3 = _adaln(cc, w_ada[0], b_ada[0]).reshape(8, N_MOD, d)

    w_in_b = w_in[0].astype(BF16)
    cos2, sin2 = _rope_tables(seq)
    tn = _tile(PROJ_COL_TILE, kv_width)
    tm_in = _tile(PROJ_ROW_TILE, seq)
    x2 = x.reshape(n_tok, d)
    common = dict(tn=tn, attn_width=attn_width, kv_width=kv_width, gate_start_col=gate_start)
    per_batch_in = seq // tm_in
    n_sections = 2
    assert in_width % (n_sections * tn) == 0 and attn_width % (2 * kv_width) == 0
    p_all = _inproj(x2, mod3, g_pre_mix, w_in_b, cos2, sin2, q_norm, k_norm, tm=tm_in,
                    sec_width=in_width // n_sections, sec_off=0, n_sections=n_sections,
                    mod_row=lambda i: i // per_batch_in, pos_blocks=per_batch_in, use_rope=True, **common)
    tm_ctx = _tile(PROJ_ROW_TILE, n_ctx)
    p_ctx = _inproj(ctx.reshape(batch * n_ctx, d), mod3, g_pre_mix, w_in_b, cos2[:tm_ctx], sin2[:tm_ctx],
                    q_norm, k_norm, tm=tm_ctx, sec_width=2 * kv_width, sec_off=attn_width // (2 * kv_width),
                    n_sections=1, mod_row=lambda i: batch, pos_blocks=1, use_rope=False, **common)

    attn = _attention(p_all, p_ctx, batch=batch, seq=seq, n_ctx=n_ctx, n_kv=n_kv, group=group,
                      tq=_tile(ATTN_QUERY_TILE, seq))

    tm_tok = _tile(TOKEN_TILE, seq)
    tm_wide = _tile(2 * TOKEN_TILE, seq)
    merged = _merge(attn, p_all, conv_w[0], w_attn_out[0].astype(BF16), w_conv_out[0].astype(BF16),
                    tm=tm_wide, seq=seq, conv_width=conv_width, d=d,
                    conv_start_col=conv_start, gate_start_col=gate_start)

    pad = LANES - n_groups - n_experts
    w_r = jnp.concatenate([w_router_group[0], w_router_expert[0].reshape(d, n_experts), jnp.zeros((d, pad), F32)], axis=1)
    b_r = jnp.concatenate([b_router_group[0], b_router_expert[0].reshape(n_experts), jnp.zeros((pad,), F32)])[None, :]
    h, f, route, counts = _mix_route(merged, w_mix_out[0].astype(BF16), x2, mod3, g_post_mix, g_pre_ffn, w_r, b_r,
                                     tm=tm_tok, seq=seq, n_groups=n_groups, per_group=per_group)

    bm = _tile(EXPERT_BLOCK_ROWS, n_tok)
    n_blocks = n_tok * TOP_K // bm + n_experts
    cnt = counts[0, :n_experts].astype(jnp.int32)
    padded = (cnt + bm - 1) // bm * bm
    pend = jnp.cumsum(padded)
    pstart = pend - padded
    e1, e2 = route[:, 0].astype(jnp.int32), route[:, 1].astype(jnp.int32)
    expert_ids = jnp.arange(n_experts, dtype=jnp.int32)[None, :]

    def segment_start(e):
        return jnp.sum(jnp.where(e[:, None] == expert_ids, pstart[None, :], 0), axis=1)

    dest1 = segment_start(e1) + route[:, 4].astype(jnp.int32)
    dest2 = segment_start(e2) + route[:, 5].astype(jnp.int32)
    n_used = (pend[-1] // bm).astype(jnp.int32)
    blk_row = jnp.minimum(jnp.arange(n_blocks, dtype=jnp.int32), n_used - 1) * bm
    blk_exp = jnp.sum((pend[None, :] <= blk_row[:, None]).astype(jnp.int32), axis=1)
    blk_exp = jnp.minimum(blk_exp, n_experts - 1)

    def per_block(table):
        return jnp.sum(jnp.where(blk_exp[:, None] == expert_ids, table[None, :], 0), axis=1)

    blk_valid = jnp.clip(per_block(pstart + cnt) - blk_row, 0, bm)
    blk_first = (blk_row == per_block(pstart)).astype(jnp.int32)
    used = cnt > 0
    later = jnp.where(used[None, :] & (expert_ids > expert_ids.T), expert_ids, n_experts)
    next_used = jnp.min(later, axis=1)
    blk_next = per_block(jnp.where(next_used < n_experts, next_used, -1))
    blk_slot = per_block((jnp.cumsum(used.astype(jnp.int32)) - 1) % 2)

    xs = _dispatch(dest1, dest2, f, n_rows=n_blocks * bm, tm=tm_wide)
    yb = _experts(blk_exp, n_used.reshape(1), blk_valid, blk_first, blk_next, blk_slot, xs,
                  w_gate[0], w_up[0], w_down[0], bm=bm)
    out = _combine(dest1, dest2, yb, route, h, mod3, g_post_ffn, tm=tm_tok, seq=seq)
    return out.reshape(batch, seq, d)
```
